```python
import math
import jax, jax.numpy as jnp
from jax import lax
import numpy as np

D_MODEL = 1024
BATCH = 2
SEQ = 8192
DEPTH = 1

N_HEADS_A = 8
HEAD_DIM = 64
WIDTH_A = N_HEADS_A * HEAD_DIM
IDX_HEADS = 8
IDX_DIM = 64
TOPK_MAX = 256
N_HEADS_B = 8
WIDTH_B = N_HEADS_B * HEAD_DIM
N_BUCKETS = 32
MAX_DISTANCE = 128
PEER_HEADS = 8
N_KEYS = 128
N_EXPERTS = N_KEYS * N_KEYS
PEER_QDIM = 256
PEER_HALF = PEER_QDIM // 2
PEER_TOPK = 16
Q_BLOCK = 128
TOK_CHUNK = 128
EPS = 1e-6
IN_SPLIT_SIZES = (WIDTH_A, WIDTH_A, WIDTH_A,
                  IDX_HEADS * IDX_DIM, IDX_DIM, IDX_HEADS,
                  WIDTH_B, WIDTH_B, WIDTH_B,
                  D_MODEL, D_MODEL)
IN_WIDTH = 3 * WIDTH_A + IDX_HEADS * IDX_DIM + IDX_DIM + IDX_HEADS + 3 * WIDTH_B + 2 * D_MODEL

kernel_name = "hybrid_dsa_stickbreak_peer_block"


def rmsnorm(x, g):
    xf = x.astype(jnp.float32)
    y = xf * lax.rsqrt(jnp.mean(xf * xf, axis=-1, keepdims=True) + EPS)
    return (y * g.astype(jnp.float32)).astype(x.dtype)


def modulate(h, shift, scale):
    return h * (1 + scale[:, None, :]) + shift[:, None, :]


def t5_bucket(n):
    max_exact = N_BUCKETS // 2
    nf = jnp.maximum(n, 1).astype(jnp.float32)
    large = max_exact + (jnp.log(nf / max_exact) / math.log(MAX_DISTANCE / max_exact)
                         * (N_BUCKETS - max_exact)).astype(jnp.int32)
    large = jnp.minimum(large, N_BUCKETS - 1)
    return jnp.where(n < max_exact, n, large)


def to_blocks(a, nblk):
    b = a.shape[0]
    return a.reshape(b, nblk, Q_BLOCK, *a.shape[2:]).swapaxes(0, 1)


def dsa_attention(q, k, v, qi, ki, wi, rel_bias):
    b, s, h, dh = q.shape
    topk = min(TOPK_MAX, s // 4)
    nblk = s // Q_BLOCK
    key_pos = jnp.arange(s)
    idx_scale = (IDX_DIM ** -0.5) * (IDX_HEADS ** -0.5)

    def one_block(args):
        blk, qb, qib, wib = args
        q_pos = blk * Q_BLOCK + jnp.arange(Q_BLOCK)
        rel = jax.nn.relu(jnp.einsum('bthd,bsd->bths', qib, ki))
        score = jnp.einsum('bths,bth->bts', rel, wib).astype(jnp.float32) * idx_scale
        causal = key_pos[None, :] <= q_pos[:, None]
        score = jnp.where(causal[None], score, -jnp.inf)
        top_score, idx = lax.top_k(score, topk)
        valid = jnp.isfinite(top_score)
        kg = jax.vmap(lambda kk, ii: kk[ii])(k, idx)
        vg = jax.vmap(lambda vv, ii: vv[ii])(v, idx)
        logits = jnp.einsum('bthd,btkhd->bhtk', qb, kg).astype(jnp.float32) * (dh ** -0.5)
        bias = rel_bias[t5_bucket(q_pos[None, :, None] - idx)].astype(jnp.float32)
        logits = logits + jnp.moveaxis(bias, -1, 1)
        logits = jnp.where(valid[:, None], logits, -jnp.inf)
        p = jax.nn.softmax(logits, axis=-1).astype(v.dtype)
        return jnp.einsum('bhtk,btkhd->bthd', p, vg)

    out = lax.map(one_block, (jnp.arange(nblk), to_blocks(q, nblk), to_blocks(qi, nblk), to_blocks(wi, nblk)))
    return out.swapaxes(0, 1).reshape(b, s, h * dh)


def stick_breaking_attention(q, k, v):
    b, s, h, dh = q.shape
    nblk = s // Q_BLOCK
    key_pos = jnp.arange(s)

    def one_block(args):
        blk, qb = args
        q_pos = blk * Q_BLOCK + jnp.arange(Q_BLOCK)
        z = jnp.einsum('bthd,bshd->bhts', qb, k).astype(jnp.float32) * (dh ** -0.5)
        strict = (key_pos[None, :] < q_pos[:, None])[None, None]
        log_beta = jax.nn.log_sigmoid(z)
        log_one_minus = jnp.where(strict, log_beta - z, 0.0)
        suffix = lax.cumsum(log_one_minus, axis=3, reverse=True) - log_one_minus
        a = jnp.where(strict, jnp.exp(log_beta + suffix), 0.0).astype(v.dtype)
        return jnp.einsum('bhts,bshd->bthd', a, v)

    out = lax.map(one_block, (jnp.arange(nblk), to_blocks(q, nblk)))
    return out.swapaxes(0, 1).reshape(b, s, h * dh)


def peer_ffn(h, w_q, sub_keys, u, v):
    t, d = h.shape
    qh = (h @ w_q).reshape(t, PEER_HEADS, 2, PEER_HALF)
    s = jnp.einsum('thcd,hcnd->thcn', qh, sub_keys).astype(jnp.float32)
    s1, i1 = lax.top_k(s[:, :, 0], PEER_TOPK)
    s2, i2 = lax.top_k(s[:, :, 1], PEER_TOPK)
    cand = (s1[..., :, None] + s2[..., None, :]).reshape(t, PEER_HEADS, PEER_TOPK * PEER_TOPK)
    cand_idx = (i1[..., :, None] * N_KEYS + i2[..., None, :]).reshape(t, PEER_HEADS, PEER_TOPK * PEER_TOPK)
    top_s, pos = lax.top_k(cand, PEER_TOPK)
    eidx = jnp.take_along_axis(cand_idx, pos, axis=-1)
    g = jax.nn.softmax(top_s, axis=-1)
    nchunk = t // TOK_CHUNK

    def one_chunk(args):
        hc, ec, gc = args
        pre = jnp.einsum('cd,chkd->chk', hc, u[ec]).astype(jnp.float32)
        coef = (gc * jax.nn.gelu(pre, approximate=False)).astype(h.dtype)
        return jnp.einsum('chk,chkd->cd', coef, v[ec])

    out = lax.map(one_chunk, (h.reshape(nchunk, TOK_CHUNK, d),
                              eidx.reshape(nchunk, TOK_CHUNK, PEER_HEADS, PEER_TOPK),
                              g.reshape(nchunk, TOK_CHUNK, PEER_HEADS, PEER_TOPK)))
    return out.reshape(t, d)


def setup_inputs(seed: int = 0) -> dict:
    key = jax.random.key(seed)
    ks = jax.random.split(key, 20)
    D = D_MODEL
    nrm = jax.random.normal
    x = nrm(ks[0], (BATCH, SEQ, D), jnp.float32)
    c = nrm(ks[1], (BATCH, D), jnp.float32)
    w_ada = nrm(ks[2], (DEPTH, D, 6 * D), jnp.float32) * (0.5 * D ** -0.5)
    b_ada = nrm(ks[3], (DEPTH, 6 * D), jnp.float32) * 0.01
    norm1_g = 1.0 + 0.05 * nrm(ks[4], (DEPTH, D), jnp.float32)
    norm2_g = 1.0 + 0.05 * nrm(ks[5], (DEPTH, D), jnp.float32)
    w_in = nrm(ks[6], (DEPTH, D, IN_WIDTH), jnp.float32) * D ** -0.5
    rel_bias = nrm(ks[7], (N_BUCKETS, N_HEADS_A), jnp.float32) * 0.5
    w_proj_a = nrm(ks[8], (DEPTH, WIDTH_A, D), jnp.float32) * WIDTH_A ** -0.5
    w_proj_b = nrm(ks[9], (DEPTH, WIDTH_B, D), jnp.float32) * WIDTH_B ** -0.5
    w_out = nrm(ks[10], (DEPTH, D, D), jnp.float32) * D ** -0.5
    peer_wq = nrm(ks[11], (DEPTH, D, PEER_HEADS * PEER_QDIM), jnp.float32) * D ** -0.5
    peer_sub_keys = nrm(ks[12], (DEPTH, PEER_HEADS, 2, N_KEYS, PEER_HALF), jnp.float32) * PEER_HALF ** -0.5
    peer_u = nrm(ks[13], (DEPTH, N_EXPERTS, D), jnp.float32) * D ** -0.5
    peer_v = nrm(ks[14], (DEPTH, N_EXPERTS, D), jnp.float32) * PEER_HEADS ** -0.5
    final_g = 1.0 + 0.05 * nrm(ks[15], (D,), jnp.float32)
    return {"x": x, "c": c, "w_ada": w_ada, "b_ada": b_ada, "norm1_g": norm1_g, "norm2_g": norm2_g,
            "w_in": w_in, "rel_bias": rel_bias, "w_proj_a": w_proj_a, "w_proj_b": w_proj_b,
            "w_out": w_out, "peer_wq": peer_wq, "peer_sub_keys": peer_sub_keys, "peer_u": peer_u,
            "peer_v": peer_v, "final_g": final_g}


def reference(x, c, w_ada, b_ada, norm1_g, norm2_g, w_in, rel_bias, w_proj_a, w_proj_b, w_out,
              peer_wq, peer_sub_keys, peer_u, peer_v, final_g):
    b, s, d = x.shape
    offsets = []
    acc = 0
    for size in IN_SPLIT_SIZES[:-1]:
        acc += size
        offsets.append(acc)
    cond = jax.nn.silu(c)
    for l in range(DEPTH):
        mod = (cond @ w_ada[l] + b_ada[l]).reshape(b, 6, d)
        shift1, scale1, gate1, shift2, scale2, gate2 = [mod[:, i] for i in range(6)]

        hn = modulate(rmsnorm(x, norm1_g[l]), shift1, scale1)
        proj = hn @ w_in[l]
        qa, ka, va, qi, ki, wi, qb, kb, vb, ga, gb = jnp.split(proj, offsets, axis=-1)
        hd = (b, s, -1, HEAD_DIM)
        ya = dsa_attention(qa.reshape(hd), ka.reshape(hd), va.reshape(hd),
                           qi.reshape(b, s, IDX_HEADS, IDX_DIM), ki, wi, rel_bias)
        yb = stick_breaking_attention(qb.reshape(hd), kb.reshape(hd), vb.reshape(hd))
        merged = jax.nn.sigmoid(ga) * (ya @ w_proj_a[l]) + jax.nn.sigmoid(gb) * (yb @ w_proj_b[l])
        x = x + gate1[:, None, :] * (merged @ w_out[l])

        hn2 = modulate(rmsnorm(x, norm2_g[l]), shift2, scale2)
        f = peer_ffn(hn2.reshape(b * s, d), peer_wq[l], peer_sub_keys[l], peer_u[l], peer_v[l]).reshape(b, s, d)
        x = x + gate2[:, None, :] * f
    return rmsnorm(x, final_g)
```

```python
import functools
import math

import jax
import jax.numpy as jnp
import numpy as np
from jax import lax
from jax.experimental import pallas as pl
from jax.experimental.pallas import tpu as pltpu

N_HEADS = 8
HEAD_DIM = 64
WIDTH = N_HEADS * HEAD_DIM
IDX_DIM = 64
TOPK_MAX = 256
N_BUCKETS = 32
MAX_DISTANCE = 128
EPS = 1e-6

V7X_LANES = 128
V7X_VMEM_LIMIT = 56 * 1024 * 1024

F32 = jnp.float32
BF16 = jnp.bfloat16


def _cparams(sem):
    return pltpu.CompilerParams(dimension_semantics=sem, vmem_limit_bytes=V7X_VMEM_LIMIT)


def _const_spec(shape):
    return pl.BlockSpec(shape, lambda *_: (0,) * len(shape), pipeline_mode=pl.Buffered(1))


def _ada_kernel(c_ref, w_ref, b_ref, o_ref):
    c = c_ref[...]
    cond = c * jax.nn.sigmoid(c)
    o_ref[...] = jnp.dot(cond, w_ref[...], preferred_element_type=F32,
                         precision=lax.Precision.HIGHEST) + b_ref[...]


def _ada(c, w, b):
    bsz, d = c.shape
    n = w.shape[1]
    tn = 1536
    return pl.pallas_call(
        _ada_kernel,
        grid=(n // tn,),
        in_specs=[pl.BlockSpec((bsz, d), lambda j: (0, 0)),
                  pl.BlockSpec((d, tn), lambda j: (0, j)),
                  pl.BlockSpec((1, tn), lambda j: (0, j))],
        out_specs=pl.BlockSpec((bsz, tn), lambda j: (0, j)),
        out_shape=jax.ShapeDtypeStruct((bsz, n), F32),
        compiler_params=_cparams(("arbitrary",)),
        name="ada",
    )(c, w, b.reshape(1, n))


def _modulated_norm(x, g, shift, scale):
    y = x * lax.rsqrt(jnp.mean(x * x, axis=-1, keepdims=True) + EPS)
    return (y * g) * (1.0 + scale) + shift


def _in_proj_kernel(x_ref, mod_ref, g_ref, wr_ref, wt_ref, ws_ref,
                    qa_ref, va_ref, qi_ref, qb_ref, vb_ref, ga_ref, gb_ref, wi_ref,
                    kat_ref, kit_ref, kbt_ref, *, idx_scale):
    h = _modulated_norm(x_ref[0], g_ref[...], mod_ref[0, 0:1, :], mod_ref[0, 1:2, :]).astype(BF16)
    w = WIDTH
    for k, ref in enumerate((qa_ref, va_ref, qi_ref, qb_ref, vb_ref)):
        ref[0] = jnp.dot(h, wr_ref[:, k * w:(k + 1) * w], preferred_element_type=F32).astype(BF16)
    d = ga_ref.shape[-1]
    for k, ref in enumerate((ga_ref, gb_ref)):
        for half in range(d // w):
            c0 = 5 * w + k * d + half * w
            ref[0, :, half * w:(half + 1) * w] = jnp.dot(h, wr_ref[:, c0:c0 + w],
                                                        preferred_element_type=F32)
    wi_ref[0] = jnp.dot(h, ws_ref[...], preferred_element_type=F32) * idx_scale
    nt = (((1,), (1,)), ((), ()))
    kat_ref[0] = lax.dot_general(wt_ref[0:w, :], h, nt, preferred_element_type=F32).astype(BF16)
    kit_ref[0] = lax.dot_general(wt_ref[w:w + 2 * IDX_DIM, :], h, nt,
                                 preferred_element_type=F32).astype(BF16)
    kbt_ref[0] = lax.dot_general(wt_ref[w + 2 * IDX_DIM:, :], h, nt,
                                 preferred_element_type=F32).astype(BF16)


def _in_proj(x, mod, g1, w_in, tm=512):
    bsz, s, d = x.shape
    w = WIDTH
    offs = np.cumsum([0, w, w, w, w, IDX_DIM, N_HEADS, w, w, w, d, d])
    col = lambda i: w_in[:, offs[i]:offs[i + 1]]
    qscale = HEAD_DIM ** -0.5
    w_rows = jnp.concatenate([col(0) * qscale, col(2), col(3), col(6) * qscale, col(8), col(9), col(10)],
                             axis=1).astype(BF16)
    w_t = jnp.concatenate([col(1), col(4), col(4), col(7)], axis=1).T.astype(BF16)
    w_s = jnp.pad(col(5), ((0, 0), (0, V7X_LANES - N_HEADS))).astype(BF16)
    idx_scale = (IDX_DIM ** -0.5) * (N_HEADS ** -0.5)

    row = lambda n, dt: jax.ShapeDtypeStruct((bsz, s, n), dt)
    tr = lambda n: jax.ShapeDtypeStruct((bsz, n, s), BF16)
    rspec = lambda n: pl.BlockSpec((1, tm, n), lambda b, i: (b, i, 0))
    tspec = lambda n: pl.BlockSpec((1, n, tm), lambda b, i: (b, 0, i))
    return pl.pallas_call(
        functools.partial(_in_proj_kernel, idx_scale=idx_scale),
        grid=(bsz, s // tm),
        in_specs=[rspec(d),
                  pl.BlockSpec((1, 6, d), lambda b, i: (b, 0, 0)),
                  _const_spec((1, d)),
                  _const_spec(w_rows.shape), _const_spec(w_t.shape), _const_spec(w_s.shape)],
        out_specs=[rspec(w)] * 5 + [rspec(d)] * 2 + [rspec(V7X_LANES)]
                  + [tspec(w), tspec(2 * IDX_DIM), tspec(w)],
        out_shape=[row(w, BF16)] * 5 + [row(d, F32)] * 2 + [row(V7X_LANES, F32)]
                  + [tr(w), tr(2 * IDX_DIM), tr(w)],
        compiler_params=_cparams(("parallel", "parallel")),
        name="in_proj",
    )(x, mod, g1.reshape(1, d), w_rows, w_t, w_s)


def _t5_bucket_np(n):
    max_exact = N_BUCKETS // 2
    nf = np.maximum(n, 1).astype(np.float32)
    large = max_exact + (np.log(nf / np.float32(max_exact)) / np.float32(math.log(MAX_DISTANCE / max_exact))
                         * np.float32(N_BUCKETS - max_exact)).astype(np.int32)
    large = np.minimum(large, N_BUCKETS - 1)
    return np.where(n < max_exact, n, large).astype(np.int32)


def _near_bias_kernel(bk_ref, rb_ref, o_ref):
    bk = bk_ref[...]
    for h in range(N_HEADS):
        far = rb_ref[N_BUCKETS - 1, h]
        acc = jnp.zeros(bk.shape, F32)
        for b in range(N_BUCKETS - 1):
            acc = jnp.where(bk == b, rb_ref[b, h] - far, acc)
        o_ref[h] = acc


def _near_bias(rel_bias, tq):
    a = np.arange(tq)[:, None]
    b = np.arange(tq + V7X_LANES)[None, :]
    n = a + V7X_LANES - b
    far_from = int(np.argmax(_t5_bucket_np(np.arange(4 * MAX_DISTANCE)) == N_BUCKETS - 1))
    assert far_from <= V7X_LANES, "distances beyond the near zone must share the last bucket"
    bucket = np.where(n >= 0, _t5_bucket_np(np.maximum(n, 0)), N_BUCKETS - 1).astype(np.int32)
    return pl.pallas_call(
        _near_bias_kernel,
        in_specs=[pl.BlockSpec(memory_space=pltpu.VMEM), pl.BlockSpec(memory_space=pltpu.SMEM)],
        out_specs=pl.BlockSpec(memory_space=pltpu.VMEM),
        out_shape=jax.ShapeDtypeStruct((N_HEADS, tq, tq + V7X_LANES), F32),
        name="near_bias",
    )(jnp.asarray(bucket), rel_bias)


INT_MIN = -2 ** 31
NEG = -1e30
IDX_BLOCK = 512
DSA_TQ = 128


def _sortable(x):
    bits = pltpu.bitcast(x, jnp.int32)
    return bits ^ ((bits >> 31) & 0x7FFFFFFF)


def _dsa_kernel(qi_ref, wi_ref, kit_ref, qa_ref, kat_ref, va_ref, nb_ref, o_ref,
                keys_ref, qis_ref, qam_ref, m_ref, l_ref, acc_ref, *, tq, topk):
    i = pl.program_id(1)
    t0 = i * tq
    lanes = V7X_LANES
    lo_half = lax.broadcasted_iota(jnp.int32, (tq, lanes), 1) < HEAD_DIM

    for h in range(N_HEADS):
        p = h // 2
        keep = lo_half if h % 2 == 0 else jnp.logical_not(lo_half)
        qis_ref[h * tq:(h + 1) * tq, :] = jnp.where(
            keep, qi_ref[0, :, p * lanes:(p + 1) * lanes].astype(F32), 0.0).astype(BF16)
        qam_ref[h] = jnp.where(keep, qa_ref[0, :, p * lanes:(p + 1) * lanes].astype(F32), 0.0).astype(BF16)

    sk = IDX_BLOCK
    n_blocks = (t0 + tq + sk - 1) // sk
    row_t = t0 + lax.broadcasted_iota(jnp.int32, (tq, sk), 0)
    col = lax.broadcasted_iota(jnp.int32, (tq, sk), 1)
    w = wi_ref[0]

    def score_body(kb, carry):
        s0 = pl.multiple_of(kb * sk, sk)
        z = jnp.dot(qis_ref[...], kit_ref[0, :, pl.ds(s0, sk)], preferred_element_type=F32)
        sc = w[:, 0:1] * jnp.maximum(z[0:tq], 0.0)
        for h in range(1, N_HEADS):
            sc = sc + w[:, h:h + 1] * jnp.maximum(z[h * tq:(h + 1) * tq], 0.0)
        keys_ref[:, pl.ds(s0, sk)] = jnp.where(col + s0 <= row_t, _sortable(sc), INT_MIN)
        return carry

    lax.fori_loop(0, n_blocks, score_body, 0)

    def bisect_body(it, r):
        bit = lax.shift_left(jnp.int32(1), 31 - it)
        cand = (r | bit) ^ INT_MIN

        def count_body(kb, acc):
            s0 = pl.multiple_of(kb * sk, sk)
            ge = jnp.where(keys_ref[:, pl.ds(s0, sk)] >= cand, 1, 0)
            for j in range(sk // lanes):
                acc = acc + ge[:, j * lanes:(j + 1) * lanes]
            return acc

        acc = lax.fori_loop(0, n_blocks, count_body, jnp.zeros((tq, lanes), jnp.int32))
        cnt = jnp.sum(acc, axis=1, keepdims=True)
        return jnp.where(cnt >= topk, r | bit, r)

    r = lax.fori_loop(0, 32, bisect_body, jnp.zeros((tq, 1), jnp.int32))
    thr = jnp.maximum(r ^ INT_MIN, INT_MIN + 1)

    m_ref[...] = jnp.full(m_ref.shape, NEG, F32)
    l_ref[...] = jnp.zeros(l_ref.shape, F32)
    acc_ref[...] = jnp.zeros(acc_ref.shape, F32)

    def attend(s0, width, bias_cols):
        maskadd = jnp.where(keys_ref[:, pl.ds(s0, width)] >= thr, 0.0, NEG)
        for h in range(N_HEADS):
            p = h // 2
            kt = kat_ref[0, p * lanes:(p + 1) * lanes, pl.ds(s0, width)]
            lg = jnp.dot(qam_ref[h], kt, preferred_element_type=F32) + maskadd
            if bias_cols is not None:
                lg = lg + nb_ref[h, :, bias_cols[0]:bias_cols[1]]
            m_old = m_ref[h]
            m_new = jnp.maximum(m_old, jnp.max(lg, axis=1, keepdims=True))
            alpha = jnp.exp(m_old - m_new)
            pr = jnp.exp(lg - m_new)
            l_ref[h] = alpha * l_ref[h] + jnp.sum(pr, axis=1, keepdims=True)
            v = va_ref[0, pl.ds(s0, width), p * lanes:(p + 1) * lanes]
            acc_ref[h] = alpha * acc_ref[h] + jnp.dot(pr.astype(BF16), v, preferred_element_type=F32)
            m_ref[h] = m_new

    far_tiles = jnp.maximum(t0 // lanes - 1, 0)
    per = sk // lanes
    n_far_big = far_tiles // per

    def far_big(kb, carry):
        attend(pl.multiple_of(kb * sk, sk), sk, None)
        return carry

    lax.fori_loop(0, n_far_big, far_big, 0)

    def far_small(kb, carry):
        attend(pl.multiple_of(kb * lanes, lanes), lanes, None)
        return carry

    lax.fori_loop(n_far_big * per, far_tiles, far_small, 0)

    @pl.when(i > 0)
    def _():
        attend(pl.multiple_of(t0 - lanes, lanes), tq + lanes, (0, tq + lanes))

    @pl.when(i == 0)
    def _():
        attend(0, tq, (lanes, tq + lanes))

    for p in range(N_HEADS // 2):
        even = acc_ref[2 * p] / l_ref[2 * p]
        odd = acc_ref[2 * p + 1] / l_ref[2 * p + 1]
        o_ref[0, :, p * lanes:(p + 1) * lanes] = jnp.where(lo_half, even, odd).astype(o_ref.dtype)


def _dsa(qi, wi, kit, qa, kat, va, near_bias, tq):
    bsz, s, w = qa.shape
    topk = min(TOPK_MAX, s // 4)
    lanes = V7X_LANES
    qspec = lambda n: pl.BlockSpec((1, tq, n), lambda b, i: (b, i, 0))
    resident = lambda shape: pl.BlockSpec((1,) + shape, lambda b, i: (b, 0, 0), pipeline_mode=pl.Buffered(1))
    return pl.pallas_call(
        functools.partial(_dsa_kernel, tq=tq, topk=topk),
        grid=(bsz, s // tq),
        in_specs=[qspec(w), qspec(lanes), resident((2 * IDX_DIM, s)),
                  qspec(w), resident((w, s)), resident((s, w)),
                  _const_spec(near_bias.shape)],
        out_specs=qspec(w),
        out_shape=jax.ShapeDtypeStruct((bsz, s, w), BF16),
        scratch_shapes=[pltpu.VMEM((tq, s), jnp.int32),
                        pltpu.VMEM((N_HEADS * tq, lanes), BF16),
                        pltpu.VMEM((N_HEADS, tq, lanes), BF16),
                        pltpu.VMEM((N_HEADS, tq, 1), F32),
                        pltpu.VMEM((N_HEADS, tq, 1), F32),
                        pltpu.VMEM((N_HEADS, tq, lanes), F32)],
        compiler_params=_cparams(("arbitrary", "arbitrary")),
        name="dsa",
    )(qi, wi, kit, qa, kat, va, near_bias)


SB_TQ = 256


def _sb_kernel(q_ref, kt_ref, v_ref, o_ref, *, tq):
    i = pl.program_id(2)
    lanes = V7X_LANES
    lo_half = lax.broadcasted_iota(jnp.int32, (tq, lanes), 1) < HEAD_DIM
    r_i = lax.broadcasted_iota(jnp.int32, (tq, tq), 0)
    c_i = lax.broadcasted_iota(jnp.int32, (tq, tq), 1)
    strict = c_i < r_i
    after = jnp.where(r_i > c_i, 1.0, 0.0).astype(BF16)
    q = q_ref[0].astype(F32)

    def tile(qm, kb, diag, carry, acc):
        s0 = pl.multiple_of(kb * tq, tq)
        z = jnp.dot(qm, kt_ref[0, :, pl.ds(s0, tq)], preferred_element_type=F32)
        sp = jnp.log1p(jnp.exp(-jnp.abs(z)))
        log_beta = jnp.minimum(z, 0.0) - sp
        log_rest = -(jnp.maximum(z, 0.0) + sp)
        if diag:
            log_rest = jnp.where(strict, log_rest, 0.0)
        hi = log_rest.astype(BF16)
        lo = (log_rest - hi.astype(F32)).astype(BF16)
        suffix = (jnp.dot(hi, after, preferred_element_type=F32)
                  + jnp.dot(lo, after, preferred_element_type=F32))
        a = jnp.exp(log_beta + suffix + carry)
        if diag:
            a = jnp.where(strict, a, 0.0)
        acc = acc + jnp.dot(a.astype(BF16), v_ref[0, pl.ds(s0, tq), :], preferred_element_type=F32)
        return carry + jnp.sum(log_rest, axis=1, keepdims=True), acc

    outs = []
    for hh in range(2):
        keep = lo_half if hh == 0 else jnp.logical_not(lo_half)
        qm = jnp.where(keep, q, 0.0).astype(BF16)
        carry, acc = tile(qm, i, True, jnp.zeros((tq, 1), F32), jnp.zeros((tq, lanes), F32))

        def body(step, ca, qm=qm):
            return tile(qm, i - 1 - step, False, *ca)

        carry, acc = lax.fori_loop(0, i, body, (carry, acc))
        outs.append(acc)
    o_ref[0] = jnp.where(lo_half, outs[0], outs[1]).astype(o_ref.dtype)


def _stick_breaking(qb, kbt, vb, tq=SB_TQ):
    bsz, s, w = qb.shape
    lanes = V7X_LANES
    return pl.pallas_call(
        functools.partial(_sb_kernel, tq=tq),
        grid=(bsz, w // lanes, s // tq),
        in_specs=[pl.BlockSpec((1, tq, lanes), lambda b, p, i: (b, i, p)),
                  pl.BlockSpec((1, lanes, s), lambda b, p, i: (b, p, 0)),
                  pl.BlockSpec((1, s, lanes), lambda b, p, i: (b, 0, p))],
        out_specs=pl.BlockSpec((1, tq, lanes), lambda b, p, i: (b, i, p)),
        out_shape=jax.ShapeDtypeStruct((bsz, s, w), BF16),
        compiler_params=_cparams(("arbitrary", "arbitrary", "arbitrary")),
        name="stick_breaking",
    )(qb, kbt, vb)


PEER_HEADS = 8
N_KEYS = 128
PEER_HALF = 128
PEER_TOPK = 16
PEER_GROUPS = 2 * PEER_HEADS


def _mix_kernel(x_ref, ya_ref, yb_ref, ga_ref, gb_ref, mod_ref, g2_ref, wa_ref, wb_ref, wo_ref, wq_ref,
                sk_ref, x1_ref, hn2t_ref, st_ref):
    ma = jnp.dot(ya_ref[0], wa_ref[...], preferred_element_type=F32)
    mb = jnp.dot(yb_ref[0], wb_ref[...], preferred_element_type=F32)
    merged = jax.nn.sigmoid(ga_ref[0]) * ma + jax.nn.sigmoid(gb_ref[0]) * mb
    x1 = x_ref[0] + mod_ref[0, 2:3, :] * jnp.dot(merged.astype(BF16), wo_ref[...], preferred_element_type=F32)
    x1_ref[0] = x1
    h2 = _modulated_norm(x1, g2_ref[...], mod_ref[0, 3:4, :], mod_ref[0, 4:5, :])
    hn2t_ref[0] = h2.T.astype(BF16)
    qh = jnp.dot(h2.astype(BF16), wq_ref[...], preferred_element_type=F32).astype(BF16)
    nt = (((1,), (1,)), ((), ()))
    for g in range(PEER_GROUPS):
        st_ref[0, g * N_KEYS:(g + 1) * N_KEYS, :] = lax.dot_general(
            sk_ref[g], qh[:, g * PEER_HALF:(g + 1) * PEER_HALF], nt, preferred_element_type=F32)


def _mix(x, ya, yb, ga, gb, mod, g2, w_proj_a, w_proj_b, w_out, peer_wq, sub_keys, tm=512):
    bsz, s, d = x.shape
    w = ya.shape[-1]
    nq = peer_wq.shape[1]
    sk = sub_keys.reshape(PEER_GROUPS, N_KEYS, PEER_HALF).astype(BF16)
    rspec = lambda n: pl.BlockSpec((1, tm, n), lambda b, i: (b, i, 0))
    tspec = lambda n: pl.BlockSpec((1, n, tm), lambda b, i: (b, 0, i))
    return pl.pallas_call(
        _mix_kernel,
        grid=(bsz, s // tm),
        in_specs=[rspec(d), rspec(w), rspec(w), rspec(d), rspec(d),
                  pl.BlockSpec((1, 6, d), lambda b, i: (b, 0, 0)), _const_spec((1, d)),
                  _const_spec((w, d)), _const_spec((w, d)), _const_spec((d, d)), _const_spec((d, nq)),
                  _const_spec(sk.shape)],
        out_specs=[rspec(d), tspec(d), tspec(PEER_GROUPS * N_KEYS)],
        out_shape=[jax.ShapeDtypeStruct((bsz, s, d), F32),
                   jax.ShapeDtypeStruct((bsz, d, s), BF16),
                   jax.ShapeDtypeStruct((bsz, PEER_GROUPS * N_KEYS, s), F32)],
        compiler_params=_cparams(("parallel", "parallel")),
        name="mix_out",
    )(x, ya, yb, ga, gb, mod, g2.reshape(1, d), w_proj_a.astype(BF16), w_proj_b.astype(BF16),
      w_out.astype(BF16), peer_wq.astype(BF16), sk)


N_STATS = 8


def _peer_select_kernel(st_ref, stats_ref, top_ref, cand_ref):
    tn = st_ref.shape[-1]

    def group_body(g, carry):
        sc = st_ref[0, pl.ds(pl.multiple_of(g * N_KEYS, N_KEYS), N_KEYS), :]
        for k in range(PEER_TOPK):
            m = jnp.max(sc, axis=0, keepdims=True)
            top_ref[g, k:k + 1, :] = m
            sc = jnp.where(sc == m, -jnp.inf, sc)
        return carry

    lax.fori_loop(0, PEER_GROUPS, group_body, 0)

    def head_body(h, carry):
        v1 = top_ref[2 * h]
        v2 = top_ref[2 * h + 1]
        for k in range(PEER_TOPK):
            cand_ref[k * PEER_TOPK:(k + 1) * PEER_TOPK, :] = v1[k:k + 1, :] + v2
        cand = cand_ref[...]
        m = None
        sc = cand
        for k in range(PEER_TOPK):
            mk = jnp.max(sc, axis=0, keepdims=True)
            m = mk if m is None else m
            sc = jnp.where(sc == mk, -jnp.inf, sc)
        thr = mk
        z = jnp.sum(jnp.where(cand >= thr, jnp.exp(cand - m), 0.0), axis=0, keepdims=True)
        rows = [v1[PEER_TOPK - 1:PEER_TOPK, :], v2[PEER_TOPK - 1:PEER_TOPK, :], thr,
                v1[0:1, :], v2[0:1, :], 1.0 / z, jnp.zeros((2, tn), F32)]
        stats_ref[0, pl.ds(pl.multiple_of(h * N_STATS, N_STATS), N_STATS), :] = jnp.concatenate(rows, axis=0)
        return carry

    lax.fori_loop(0, PEER_HEADS, head_body, 0)


def _peer_select(st, tn=512):
    bsz, rows, s = st.shape
    return pl.pallas_call(
        _peer_select_kernel,
        grid=(bsz, s // tn),
        in_specs=[pl.BlockSpec((1, rows, tn), lambda b, i: (b, 0, i))],
        out_specs=pl.BlockSpec((1, PEER_HEADS * N_STATS, tn), lambda b, i: (b, 0, i)),
        out_shape=jax.ShapeDtypeStruct((bsz, PEER_HEADS * N_STATS, s), F32),
        scratch_shapes=[pltpu.VMEM((PEER_GROUPS, PEER_TOPK, tn), F32),
                        pltpu.VMEM((PEER_TOPK * PEER_TOPK, tn), F32)],
        compiler_params=_cparams(("parallel", "parallel")),
        name="peer_select",
    )(st)


def _peer_experts_kernel(ht_ref, st_ref, stats_ref, u_ref, vt_ref, o_ref, a1_ref, e1_ref, a2_ref, e2_ref,
                         pre_ref, coef_ref, *, te):
    j = pl.program_id(2)
    tm = ht_ref.shape[-1]
    lanes = V7X_LANES

    @pl.when(j == 0)
    def _():
        o_ref[...] = jnp.zeros(o_ref.shape, F32)
        for h in range(PEER_HEADS):
            st = lambda k: stats_ref[0, h * N_STATS + k:h * N_STATS + k + 1, :]
            s1 = st_ref[0, (2 * h) * N_KEYS:(2 * h + 1) * N_KEYS, :]
            s2 = st_ref[0, (2 * h + 1) * N_KEYS:(2 * h + 2) * N_KEYS, :]
            in1 = s1 >= st(0)
            in2 = s2 >= st(1)
            a1_ref[h] = jnp.where(in1, s1, NEG)
            e1_ref[h] = jnp.where(in1, jnp.exp(s1 - st(3)) * st(5), 0.0)
            a2_ref[h] = jnp.where(in2, s2, NEG)
            e2_ref[h] = jnp.where(in2, jnp.exp(s2 - st(4)), 0.0)

    pre = jnp.dot(u_ref[...], ht_ref[0], preferred_element_type=F32)
    pre_ref[...] = jnp.dot(u_ref[...], ht_ref[0], preferred_element_type=F32)
    rows_per_step = te // N_KEYS
    first = pl.multiple_of(j * rows_per_step, rows_per_step)

    def chunk_body(c, carry):
        cs = pl.ds(pl.multiple_of(c * lanes, lanes), lanes)
        for r in range(rows_per_step):
            gate = jnp.zeros((N_KEYS, lanes), F32)
            for h in range(PEER_HEADS):
                ssum = a1_ref[h, pl.ds(first, rows_per_step), cs][r:r + 1, :] + a2_ref[h, :, cs]
                wgt = e1_ref[h, pl.ds(first, rows_per_step), cs][r:r + 1, :] * e2_ref[h, :, cs]
                thr = stats_ref[0, h * N_STATS + 2:h * N_STATS + 3, cs]
                gate = gate + jnp.where(ssum >= thr, wgt, 0.0)
            x = pre_ref[r * N_KEYS:(r + 1) * N_KEYS, cs]
            gelu = 0.5 * x * (1.0 + lax.erf(x * np.float32(math.sqrt(0.5))))
            coef_ref[r * N_KEYS:(r + 1) * N_KEYS, cs] = (gate * gelu).astype(BF16)
        return carry

    lax.fori_loop(0, tm // lanes, chunk_body, 0)
    o_ref[0] += jnp.dot(vt_ref[...], coef_ref[...], preferred_element_type=F32)


def _peer_experts(hn2t, st, stats, peer_u, peer_v, tm=512, te=1024):
    bsz, d, s = hn2t.shape
    n_exp = peer_u.shape[0]
    u = peer_u.astype(BF16)
    vt = peer_v.T.astype(BF16)
    tok = lambda n: pl.BlockSpec((1, n, tm), lambda b, i, j: (b, 0, i))
    return pl.pallas_call(
        functools.partial(_peer_experts_kernel, te=te),
        grid=(bsz, s // tm, n_exp // te),
        in_specs=[tok(d), tok(st.shape[1]), tok(stats.shape[1]),
                  pl.BlockSpec((te, d), lambda b, i, j: (j, 0)),
                  pl.BlockSpec((d, te), lambda b, i, j: (0, j))],
        out_specs=tok(d),
        out_shape=jax.ShapeDtypeStruct((bsz, d, s), F32),
        scratch_shapes=[pltpu.VMEM((PEER_HEADS, N_KEYS, tm), F32)] * 4
                       + [pltpu.VMEM((te, tm), F32), pltpu.VMEM((te, tm), BF16)],
        compiler_params=_cparams(("parallel", "parallel", "arbitrary")),
        name="peer_experts",
    )(hn2t, st, stats, u, vt)


def _final_kernel(x1_ref, ft_ref, mod_ref, g_ref, o_ref):
    x2 = x1_ref[0] + mod_ref[0, 5:6, :] * ft_ref[0].T
    y = x2 * lax.rsqrt(jnp.mean(x2 * x2, axis=-1, keepdims=True) + EPS)
    o_ref[0] = y * g_ref[...]


def _final(x1, ft, mod, final_g, tm=512):
    bsz, s, d = x1.shape
    return pl.pallas_call(
        _final_kernel,
        grid=(bsz, s // tm),
        in_specs=[pl.BlockSpec((1, tm, d), lambda b, i: (b, i, 0)),
                  pl.BlockSpec((1, d, tm), lambda b, i: (b, 0, i)),
                  pl.BlockSpec((1, 6, d), lambda b, i: (b, 0, 0)), _const_spec((1, d))],
        out_specs=pl.BlockSpec((1, tm, d), lambda b, i: (b, i, 0)),
        out_shape=jax.ShapeDtypeStruct((bsz, s, d), F32),
        compiler_params=_cparams(("parallel", "parallel")),
        name="final_norm",
    )(x1, ft, mod, final_g.reshape(1, d))


def kernel(x, c, w_ada, b_ada, norm1_g, norm2_g, w_in, rel_bias, w_proj_a, w_proj_b, w_out,
           peer_wq, peer_sub_keys, peer_u, peer_v, final_g):
    bsz, s, d = x.shape
    mod = _ada(c, w_ada[0], b_ada[0]).reshape(bsz, 6, d)
    qa, va, qi, qb, vb, ga, gb, wi, kat, kit, kbt = _in_proj(x, mod, norm1_g[0], w_in[0])
    ya = _dsa(qi, wi, kit, qa, kat, va, _near_bias(rel_bias, DSA_TQ), DSA_TQ)
    yb = _stick_breaking(qb, kbt, vb)
    x1, hn2t, st = _mix(x, ya, yb, ga, gb, mod, norm2_g[0], w_proj_a[0], w_proj_b[0], w_out[0],
                        peer_wq[0], peer_sub_keys[0])
    stats = _peer_select(st)
    ft = _peer_experts(hn2t, st, stats, peer_u[0], peer_v[0])
    return _final(x1, ft, mod, final_g)
```

```python
import functools
import math

import jax
import jax.numpy as jnp
import numpy as np
from jax import lax
from jax.experimental import pallas as pl
from jax.experimental.pallas import tpu as pltpu

N_HEADS = 8
HEAD_DIM = 64
WIDTH = N_HEADS * HEAD_DIM
IDX_DIM = 64
TOPK_MAX = 256
N_BUCKETS = 32
MAX_DISTANCE = 128
EPS = 1e-6

V7X_LANES = 128
V7X_VMEM_LIMIT = 56 * 1024 * 1024

F32 = jnp.float32
BF16 = jnp.bfloat16


def _cparams(sem):
    return pltpu.CompilerParams(dimension_semantics=sem, vmem_limit_bytes=V7X_VMEM_LIMIT)


def _const_spec(shape):
    return pl.BlockSpec(shape, lambda *_: (0,) * len(shape), pipeline_mode=pl.Buffered(1))


def _ada_kernel(c_ref, w_ref, b_ref, o_ref):
    c = c_ref[...]
    cond = c * jax.nn.sigmoid(c)
    o_ref[...] = jnp.dot(cond, w_ref[...], preferred_element_type=F32,
                         precision=lax.Precision.HIGHEST) + b_ref[...]


def _ada(c, w, b):
    bsz, d = c.shape
    n = w.shape[1]
    tn = 1536
    return pl.pallas_call(
        _ada_kernel,
        grid=(n // tn,),
        in_specs=[pl.BlockSpec((bsz, d), lambda j: (0, 0)),
                  pl.BlockSpec((d, tn), lambda j: (0, j)),
                  pl.BlockSpec((1, tn), lambda j: (0, j))],
        out_specs=pl.BlockSpec((bsz, tn), lambda j: (0, j)),
        out_shape=jax.ShapeDtypeStruct((bsz, n), F32),
        compiler_params=_cparams(("arbitrary",)),
        name="ada",
    )(c, w, b.reshape(1, n))


def _modulated_norm(x, g, shift, scale):
    y = x * lax.rsqrt(jnp.mean(x * x, axis=-1, keepdims=True) + EPS)
    return (y * g) * (1.0 + scale) + shift


def _in_proj_kernel(x_ref, mod_ref, g_ref, wr_ref, wt_ref, ws_ref,
                    qa_ref, va_ref, qi_ref, qb_ref, vb_ref, ga_ref, gb_ref, wi_ref,
                    kat_ref, kit_ref, kbt_ref, *, idx_scale):
    h = _modulated_norm(x_ref[0], g_ref[...], mod_ref[0, 0:1, :], mod_ref[0, 1:2, :]).astype(BF16)
    w = WIDTH
    for k, ref in enumerate((qa_ref, va_ref, qi_ref, qb_ref, vb_ref)):
        ref[0] = jnp.dot(h, wr_ref[:, k * w:(k + 1) * w], preferred_element_type=F32).astype(BF16)
    d = ga_ref.shape[-1]
    for k, ref in enumerate((ga_ref, gb_ref)):
        for half in range(d // w):
            c0 = 5 * w + k * d + half * w
            ref[0, :, half * w:(half + 1) * w] = jnp.dot(h, wr_ref[:, c0:c0 + w],
                                                        preferred_element_type=F32)
    wi_ref[0] = jnp.dot(h, ws_ref[...], preferred_element_type=F32) * idx_scale
    nt = (((1,), (1,)), ((), ()))
    kat_ref[0] = lax.dot_general(wt_ref[0:w, :], h, nt, preferred_element_type=F32).astype(BF16)
    kit_ref[0] = lax.dot_general(wt_ref[w:w + 2 * IDX_DIM, :], h, nt,
                                 preferred_element_type=F32).astype(BF16)
    kbt_ref[0] = lax.dot_general(wt_ref[w + 2 * IDX_DIM:, :], h, nt,
                                 preferred_element_type=F32).astype(BF16)


def _in_proj(x, mod, g1, w_in, tm=512):
    bsz, s, d = x.shape
    w = WIDTH
    offs = np.cumsum([0, w, w, w, w, IDX_DIM, N_HEADS, w, w, w, d, d])
    col = lambda i: w_in[:, offs[i]:offs[i + 1]]
    qscale = HEAD_DIM ** -0.5
    w_rows = jnp.concatenate([col(0) * qscale, col(2), col(3), col(6) * qscale, col(8), col(9), col(10)],
                             axis=1).astype(BF16)
    w_t = jnp.concatenate([col(1), col(4), col(4), col(7)], axis=1).T.astype(BF16)
    w_s = jnp.pad(col(5), ((0, 0), (0, V7X_LANES - N_HEADS))).astype(BF16)
    idx_scale = (IDX_DIM ** -0.5) * (N_HEADS ** -0.5)

    row = lambda n, dt: jax.ShapeDtypeStruct((bsz, s, n), dt)
    tr = lambda n: jax.ShapeDtypeStruct((bsz, n, s), BF16)
    rspec = lambda n: pl.BlockSpec((1, tm, n), lambda b, i: (b, i, 0))
    tspec = lambda n: pl.BlockSpec((1, n, tm), lambda b, i: (b, 0, i))
    return pl.pallas_call(
        functools.partial(_in_proj_kernel, idx_scale=idx_scale),
        grid=(bsz, s // tm),
        in_specs=[rspec(d),
                  pl.BlockSpec((1, 6, d), lambda b, i: (b, 0, 0)),
                  _const_spec((1, d)),
                  _const_spec(w_rows.shape), _const_spec(w_t.shape), _const_spec(w_s.shape)],
        out_specs=[rspec(w)] * 5 + [rspec(d)] * 2 + [rspec(V7X_LANES)]
                  + [tspec(w), tspec(2 * IDX_DIM), tspec(w)],
        out_shape=[row(w, BF16)] * 5 + [row(d, F32)] * 2 + [row(V7X_LANES, F32)]
                  + [tr(w), tr(2 * IDX_DIM), tr(w)],
        compiler_params=_cparams(("parallel", "parallel")),
        name="in_proj",
    )(x, mod, g1.reshape(1, d), w_rows, w_t, w_s)


def _t5_bucket_np(n):
    max_exact = N_BUCKETS // 2
    nf = np.maximum(n, 1).astype(np.float32)
    large = max_exact + (np.log(nf / np.float32(max_exact)) / np.float32(math.log(MAX_DISTANCE / max_exact))
                         * np.float32(N_BUCKETS - max_exact)).astype(np.int32)
    large = np.minimum(large, N_BUCKETS - 1)
    return np.where(n < max_exact, n, large).astype(np.int32)


def _near_bias_kernel(bk_ref, rb_ref, o_ref):
    bk = bk_ref[...]
    for h in range(N_HEADS):
        far = rb_ref[N_BUCKETS - 1, h]
        acc = jnp.zeros(bk.shape, F32)
        for b in range(N_BUCKETS - 1):
            acc = jnp.where(bk == b, rb_ref[b, h] - far, acc)
        o_ref[h] = acc


def _near_bias(rel_bias, tq):
    a = np.arange(tq)[:, None]
    b = np.arange(tq + V7X_LANES)[None, :]
    n = a + V7X_LANES - b
    far_from = int(np.argmax(_t5_bucket_np(np.arange(4 * MAX_DISTANCE)) == N_BUCKETS - 1))
    assert far_from <= V7X_LANES, "distances beyond the near zone must share the last bucket"
    bucket = np.where(n >= 0, _t5_bucket_np(np.maximum(n, 0)), N_BUCKETS - 1).astype(np.int32)
    return pl.pallas_call(
        _near_bias_kernel,
        in_specs=[pl.BlockSpec(memory_space=pltpu.VMEM), pl.BlockSpec(memory_space=pltpu.SMEM)],
        out_specs=pl.BlockSpec(memory_space=pltpu.VMEM),
        out_shape=jax.ShapeDtypeStruct((N_HEADS, tq, tq + V7X_LANES), F32),
        name="near_bias",
    )(jnp.asarray(bucket), rel_bias)


INT_MIN = -2 ** 31
NEG = -1e30
IDX_BLOCK = 512
DSA_TQ = 128


def _sortable(x):
    bits = pltpu.bitcast(x, jnp.int32)
    return bits ^ ((bits >> 31) & 0x7FFFFFFF)


def _dsa_kernel(qi_ref, wi_ref, kit_ref, qa_ref, kat_ref, va_ref, nb_ref, o_ref,
                keys_ref, qis_ref, qam_ref, m_ref, l_ref, acc_ref, *, tq, topk):
    i = pl.program_id(1)
    t0 = i * tq
    lanes = V7X_LANES
    lo_half = lax.broadcasted_iota(jnp.int32, (tq, lanes), 1) < HEAD_DIM

    for h in range(N_HEADS):
        p = h // 2
        keep = lo_half if h % 2 == 0 else jnp.logical_not(lo_half)
        qis_ref[h * tq:(h + 1) * tq, :] = jnp.where(
            keep, qi_ref[0, :, p * lanes:(p + 1) * lanes].astype(F32), 0.0).astype(BF16)
        qam_ref[h] = jnp.where(keep, qa_ref[0, :, p * lanes:(p + 1) * lanes].astype(F32), 0.0).astype(BF16)

    sk = IDX_BLOCK
    n_blocks = (t0 + tq + sk - 1) // sk
    row_t = t0 + lax.broadcasted_iota(jnp.int32, (tq, sk), 0)
    col = lax.broadcasted_iota(jnp.int32, (tq, sk), 1)
    w = wi_ref[0]

    def score_body(kb, carry):
        s0 = pl.multiple_of(kb * sk, sk)
        z = jnp.dot(qis_ref[...], kit_ref[0, :, pl.ds(s0, sk)], preferred_element_type=F32)
        sc = w[:, 0:1] * jnp.maximum(z[0:tq], 0.0)
        for h in range(1, N_HEADS):
            sc = sc + w[:, h:h + 1] * jnp.maximum(z[h * tq:(h + 1) * tq], 0.0)
        keys_ref[:, pl.ds(s0, sk)] = jnp.where(col + s0 <= row_t, _sortable(sc), INT_MIN)
        return carry

    lax.fori_loop(0, n_blocks, score_body, 0)

    def bisect_body(it, r):
        bit = lax.shift_left(jnp.int32(1), 31 - it)
        cand = (r | bit) ^ INT_MIN

        def count_body(kb, acc):
            s0 = pl.multiple_of(kb * sk, sk)
            ge = jnp.where(keys_ref[:, pl.ds(s0, sk)] >= cand, 1, 0)
            for j in range(sk // lanes):
                acc = acc + ge[:, j * lanes:(j + 1) * lanes]
            return acc

        acc = lax.fori_loop(0, n_blocks, count_body, jnp.zeros((tq, lanes), jnp.int32))
        cnt = jnp.sum(acc, axis=1, keepdims=True)
        return jnp.where(cnt >= topk, r | bit, r)

    r = lax.fori_loop(0, 32, bisect_body, jnp.zeros((tq, 1), jnp.int32))
    thr = jnp.maximum(r ^ INT_MIN, INT_MIN + 1)

    m_ref[...] = jnp.full(m_ref.shape, NEG, F32)
    l_ref[...] = jnp.zeros(l_ref.shape, F32)
    acc_ref[...] = jnp.zeros(acc_ref.shape, F32)

    def attend(s0, width, bias_cols):
        maskadd = jnp.where(keys_ref[:, pl.ds(s0, width)] >= thr, 0.0, NEG)
        for h in range(N_HEADS):
            p = h // 2
            kt = kat_ref[0, p * lanes:(p + 1) * lanes, pl.ds(s0, width)]
            lg = jnp.dot(qam_ref[h], kt, preferred_element_type=F32) + maskadd
            if bias_cols is not None:
                lg = lg + nb_ref[h, :, bias_cols[0]:bias_cols[1]]
            m_old = m_ref[h]
            m_new = jnp.maximum(m_old, jnp.max(lg, axis=1, keepdims=True))
            alpha = jnp.exp(m_old - m_new)
            pr = jnp.exp(lg - m_new)
            l_ref[h] = alpha * l_ref[h] + jnp.sum(pr, axis=1, keepdims=True)
            v = va_ref[0, pl.ds(s0, width), p * lanes:(p + 1) * lanes]
            acc_ref[h] = alpha * acc_ref[h] + jnp.dot(pr.astype(BF16), v, preferred_element_type=F32)
            m_ref[h] = m_new

    far_tiles = jnp.maximum(t0 // lanes - 1, 0)
    per = sk // lanes
    n_far_big = far_tiles // per

    def far_big(kb, carry):
        attend(pl.multiple_of(kb * sk, sk), sk, None)
        return carry

    lax.fori_loop(0, n_far_big, far_big, 0)

    def far_small(kb, carry):
        attend(pl.multiple_of(kb * lanes, lanes), lanes, None)
        return carry

    lax.fori_loop(n_far_big * per, far_tiles, far_small, 0)

    @pl.when(i > 0)
    def _():
        attend(pl.multiple_of(t0 - lanes, lanes), tq + lanes, (0, tq + lanes))

    @pl.when(i == 0)
    def _():
        attend(0, tq, (lanes, tq + lanes))

    for p in range(N_HEADS // 2):
        even = acc_ref[2 * p] / l_ref[2 * p]
        odd = acc_ref[2 * p + 1] / l_ref[2 * p + 1]
        o_ref[0, :, p * lanes:(p + 1) * lanes] = jnp.where(lo_half, even, odd).astype(o_ref.dtype)


def _dsa(qi, wi, kit, qa, kat, va, near_bias, tq):
    bsz, s, w = qa.shape
    topk = min(TOPK_MAX, s // 4)
    lanes = V7X_LANES
    qspec = lambda n: pl.BlockSpec((1, tq, n), lambda b, i: (b, i, 0))
    resident = lambda shape: pl.BlockSpec((1,) + shape, lambda b, i: (b, 0, 0), pipeline_mode=pl.Buffered(1))
    return pl.pallas_call(
        functools.partial(_dsa_kernel, tq=tq, topk=topk),
        grid=(bsz, s // tq),
        in_specs=[qspec(w), qspec(lanes), resident((2 * IDX_DIM, s)),
                  qspec(w), resident((w, s)), resident((s, w)),
                  _const_spec(near_bias.shape)],
        out_specs=qspec(w),
        out_shape=jax.ShapeDtypeStruct((bsz, s, w), BF16),
        scratch_shapes=[pltpu.VMEM((tq, s), jnp.int32),
                        pltpu.VMEM((N_HEADS * tq, lanes), BF16),
                        pltpu.VMEM((N_HEADS, tq, lanes), BF16),
                        pltpu.VMEM((N_HEADS, tq, 1), F32),
                        pltpu.VMEM((N_HEADS, tq, 1), F32),
                        pltpu.VMEM((N_HEADS, tq, lanes), F32)],
        compiler_params=_cparams(("arbitrary", "arbitrary")),
        name="dsa",
    )(qi, wi, kit, qa, kat, va, near_bias)


SB_TQ = 256
SB_DEAD = -104.0


def _sb_kernel(q_ref, kt_ref, v_ref, o_ref, *, tq):
    i = pl.program_id(2)
    lanes = V7X_LANES
    lo_half = lax.broadcasted_iota(jnp.int32, (tq, lanes), 1) < HEAD_DIM
    r_i = lax.broadcasted_iota(jnp.int32, (tq, tq), 0)
    c_i = lax.broadcasted_iota(jnp.int32, (tq, tq), 1)
    strict = c_i < r_i
    after = jnp.where(r_i > c_i, 1.0, 0.0).astype(BF16)
    after2 = jnp.concatenate([after, after], axis=0)
    q = q_ref[0].astype(F32)
    qms = [jnp.where(lo_half, q, 0.0).astype(BF16), jnp.where(lo_half, 0.0, q).astype(BF16)]

    def tile(kb, diag, state):
        s0 = pl.multiple_of(kb * tq, tq)
        kt = kt_ref[0, :, pl.ds(s0, tq)]
        v = v_ref[0, pl.ds(s0, tq), :]
        carries, accs = [], []
        for hh in range(2):
            carry, acc = state[hh], state[2 + hh]
            z = jnp.dot(qms[hh], kt, preferred_element_type=F32)
            sp = jnp.log(1.0 + jnp.exp(-jnp.abs(z)))
            log_beta = jnp.minimum(z, 0.0) - sp
            log_rest = -(jnp.maximum(z, 0.0) + sp)
            if diag:
                log_rest = jnp.where(strict, log_rest, 0.0)
            hi = log_rest.astype(BF16)
            lo = (log_rest - hi.astype(F32)).astype(BF16)
            suffix = jnp.dot(jnp.concatenate([hi, lo], axis=1), after2, preferred_element_type=F32)
            a = jnp.exp(log_beta + suffix + carry)
            if diag:
                a = jnp.where(strict, a, 0.0)
            accs.append(acc + jnp.dot(a.astype(BF16), v, preferred_element_type=F32))
            carries.append(carry + jnp.sum(log_rest, axis=1, keepdims=True))
        return tuple(carries) + tuple(accs)

    zc = jnp.zeros((tq, 1), F32)
    za = jnp.zeros((tq, lanes), F32)
    state = tile(i, True, (zc, zc, za, za))

    def alive(ls):
        kb, state = ls
        worst = jnp.max(jnp.maximum(state[0], state[1]))
        return jnp.logical_and(kb >= 0, worst >= SB_DEAD)

    def body(ls):
        kb, state = ls
        return kb - 1, tile(kb, False, state)

    _, state = lax.while_loop(alive, body, (i - 1, state))
    o_ref[0] = jnp.where(lo_half, state[2], state[3]).astype(o_ref.dtype)


def _stick_breaking(qb, kbt, vb, tq=SB_TQ):
    bsz, s, w = qb.shape
    lanes = V7X_LANES
    return pl.pallas_call(
        functools.partial(_sb_kernel, tq=tq),
        grid=(bsz, w // lanes, s // tq),
        in_specs=[pl.BlockSpec((1, tq, lanes), lambda b, p, i: (b, i, p)),
                  pl.BlockSpec((1, lanes, s), lambda b, p, i: (b, p, 0)),
                  pl.BlockSpec((1, s, lanes), lambda b, p, i: (b, 0, p))],
        out_specs=pl.BlockSpec((1, tq, lanes), lambda b, p, i: (b, i, p)),
        out_shape=jax.ShapeDtypeStruct((bsz, s, w), BF16),
        compiler_params=_cparams(("arbitrary", "arbitrary", "arbitrary")),
        name="stick_breaking",
    )(qb, kbt, vb)


PEER_HEADS = 8
N_KEYS = 128
PEER_HALF = 128
PEER_TOPK = 16
PEER_GROUPS = 2 * PEER_HEADS


def _mix_kernel(x_ref, ya_ref, yb_ref, ga_ref, gb_ref, mod_ref, g2_ref, wa_ref, wb_ref, wo_ref, wq_ref,
                sk_ref, x1_ref, hn2t_ref, st_ref):
    ma = jnp.dot(ya_ref[0], wa_ref[...], preferred_element_type=F32)
    mb = jnp.dot(yb_ref[0], wb_ref[...], preferred_element_type=F32)
    merged = jax.nn.sigmoid(ga_ref[0]) * ma + jax.nn.sigmoid(gb_ref[0]) * mb
    x1 = x_ref[0] + mod_ref[0, 2:3, :] * jnp.dot(merged.astype(BF16), wo_ref[...], preferred_element_type=F32)
    x1_ref[0] = x1
    h2 = _modulated_norm(x1, g2_ref[...], mod_ref[0, 3:4, :], mod_ref[0, 4:5, :])
    hn2t_ref[0] = h2.T.astype(BF16)
    qh = jnp.dot(h2.astype(BF16), wq_ref[...], preferred_element_type=F32).astype(BF16)
    nt = (((1,), (1,)), ((), ()))
    for g in range(PEER_GROUPS):
        st_ref[0, g * N_KEYS:(g + 1) * N_KEYS, :] = lax.dot_general(
            sk_ref[g], qh[:, g * PEER_HALF:(g + 1) * PEER_HALF], nt, preferred_element_type=F32)


def _mix(x, ya, yb, ga, gb, mod, g2, w_proj_a, w_proj_b, w_out, peer_wq, sub_keys, tm=512):
    bsz, s, d = x.shape
    w = ya.shape[-1]
    nq = peer_wq.shape[1]
    sk = sub_keys.reshape(PEER_GROUPS, N_KEYS, PEER_HALF).astype(BF16)
    rspec = lambda n: pl.BlockSpec((1, tm, n), lambda b, i: (b, i, 0))
    tspec = lambda n: pl.BlockSpec((1, n, tm), lambda b, i: (b, 0, i))
    return pl.pallas_call(
        _mix_kernel,
        grid=(bsz, s // tm),
        in_specs=[rspec(d), rspec(w), rspec(w), rspec(d), rspec(d),
                  pl.BlockSpec((1, 6, d), lambda b, i: (b, 0, 0)), _const_spec((1, d)),
                  _const_spec((w, d)), _const_spec((w, d)), _const_spec((d, d)), _const_spec((d, nq)),
                  _const_spec(sk.shape)],
        out_specs=[rspec(d), tspec(d), tspec(PEER_GROUPS * N_KEYS)],
        out_shape=[jax.ShapeDtypeStruct((bsz, s, d), F32),
                   jax.ShapeDtypeStruct((bsz, d, s), BF16),
                   jax.ShapeDtypeStruct((bsz, PEER_GROUPS * N_KEYS, s), F32)],
        compiler_params=_cparams(("parallel", "parallel")),
        name="mix_out",
    )(x, ya, yb, ga, gb, mod, g2.reshape(1, d), w_proj_a.astype(BF16), w_proj_b.astype(BF16),
      w_out.astype(BF16), peer_wq.astype(BF16), sk)


N_STATS = 8


def _peer_select_kernel(st_ref, stats_ref, top_ref, cand_ref):
    tn = st_ref.shape[-1]

    def group_body(g, carry):
        sc = st_ref[0, pl.ds(pl.multiple_of(g * N_KEYS, N_KEYS), N_KEYS), :]
        for k in range(PEER_TOPK):
            m = jnp.max(sc, axis=0, keepdims=True)
            top_ref[g, k:k + 1, :] = m
            sc = jnp.where(sc == m, -jnp.inf, sc)
        return carry

    lax.fori_loop(0, PEER_GROUPS, group_body, 0)

    def head_body(h, carry):
        v1 = top_ref[2 * h]
        v2 = top_ref[2 * h + 1]
        for k in range(PEER_TOPK):
            cand_ref[k * PEER_TOPK:(k + 1) * PEER_TOPK, :] = v1[k:k + 1, :] + v2
        cand = cand_ref[...]
        m = None
        sc = cand
        for k in range(PEER_TOPK):
            mk = jnp.max(sc, axis=0, keepdims=True)
            m = mk if m is None else m
            sc = jnp.where(sc == mk, -jnp.inf, sc)
        thr = mk
        z = jnp.sum(jnp.where(cand >= thr, jnp.exp(cand - m), 0.0), axis=0, keepdims=True)
        rows = [v1[PEER_TOPK - 1:PEER_TOPK, :], v2[PEER_TOPK - 1:PEER_TOPK, :], thr,
                v1[0:1, :], v2[0:1, :], 1.0 / z, jnp.zeros((2, tn), F32)]
        stats_ref[0, pl.ds(pl.multiple_of(h * N_STATS, N_STATS), N_STATS), :] = jnp.concatenate(rows, axis=0)
        return carry

    lax.fori_loop(0, PEER_HEADS, head_body, 0)


def _peer_select(st, tn=512):
    bsz, rows, s = st.shape
    return pl.pallas_call(
        _peer_select_kernel,
        grid=(bsz, s // tn),
        in_specs=[pl.BlockSpec((1, rows, tn), lambda b, i: (b, 0, i))],
        out_specs=pl.BlockSpec((1, PEER_HEADS * N_STATS, tn), lambda b, i: (b, 0, i)),
        out_shape=jax.ShapeDtypeStruct((bsz, PEER_HEADS * N_STATS, s), F32),
        scratch_shapes=[pltpu.VMEM((PEER_GROUPS, PEER_TOPK, tn), F32),
                        pltpu.VMEM((PEER_TOPK * PEER_TOPK, tn), F32)],
        compiler_params=_cparams(("parallel", "parallel")),
        name="peer_select",
    )(st)


def _peer_experts_kernel(ht_ref, st_ref, stats_ref, u_ref, vt_ref, o_ref, a1_ref, e1_ref, a2_ref, e2_ref,
                         pre_ref, coef_ref, *, te):
    j = pl.program_id(2)
    tm = ht_ref.shape[-1]
    lanes = V7X_LANES

    @pl.when(j == 0)
    def _():
        o_ref[...] = jnp.zeros(o_ref.shape, F32)
        for h in range(PEER_HEADS):
            st = lambda k: stats_ref[0, h * N_STATS + k:h * N_STATS + k + 1, :]
            s1 = st_ref[0, (2 * h) * N_KEYS:(2 * h + 1) * N_KEYS, :]
            s2 = st_ref[0, (2 * h + 1) * N_KEYS:(2 * h + 2) * N_KEYS, :]
            in1 = s1 >= st(0)
            in2 = s2 >= st(1)
            a1_ref[h] = jnp.where(in1, s1, NEG)
            e1_ref[h] = jnp.where(in1, jnp.exp(s1 - st(3)) * st(5), 0.0)
            a2_ref[h] = jnp.where(in2, s2, NEG)
            e2_ref[h] = jnp.where(in2, jnp.exp(s2 - st(4)), 0.0)

    pre = jnp.dot(u_ref[...], ht_ref[0], preferred_element_type=F32)
    pre_ref[...] = jnp.dot(u_ref[...], ht_ref[0], preferred_element_type=F32)
    rows_per_step = te // N_KEYS
    first = pl.multiple_of(j * rows_per_step, rows_per_step)

    def chunk_body(c, carry):
        cs = pl.ds(pl.multiple_of(c * lanes, lanes), lanes)
        for r in range(rows_per_step):
            gate = jnp.zeros((N_KEYS, lanes), F32)
            for h in range(PEER_HEADS):
                ssum = a1_ref[h, pl.ds(first, rows_per_step), cs][r:r + 1, :] + a2_ref[h, :, cs]
                wgt = e1_ref[h, pl.ds(first, rows_per_step), cs][r:r + 1, :] * e2_ref[h, :, cs]
                thr = stats_ref[0, h * N_STATS + 2:h * N_STATS + 3, cs]
                gate = gate + jnp.where(ssum >= thr, wgt, 0.0)
            x = pre_ref[r * N_KEYS:(r + 1) * N_KEYS, cs]
            gelu = 0.5 * x * (1.0 + lax.erf(x * np.float32(math.sqrt(0.5))))
            coef_ref[r * N_KEYS:(r + 1) * N_KEYS, cs] = (gate * gelu).astype(BF16)
        return carry

    lax.fori_loop(0, tm // lanes, chunk_body, 0)
    o_ref[0] += jnp.dot(vt_ref[...], coef_ref[...], preferred_element_type=F32)


def _peer_experts(hn2t, st, stats, peer_u, peer_v, tm=512, te=1024):
    bsz, d, s = hn2t.shape
    n_exp = peer_u.shape[0]
    u = peer_u.astype(BF16)
    vt = peer_v.T.astype(BF16)
    tok = lambda n: pl.BlockSpec((1, n, tm), lambda b, i, j: (b, 0, i))
    return pl.pallas_call(
        functools.partial(_peer_experts_kernel, te=te),
        grid=(bsz, s // tm, n_exp // te),
        in_specs=[tok(d), tok(st.shape[1]), tok(stats.shape[1]),
                  pl.BlockSpec((te, d), lambda b, i, j: (j, 0)),
                  pl.BlockSpec((d, te), lambda b, i, j: (0, j))],
        out_specs=tok(d),
        out_shape=jax.ShapeDtypeStruct((bsz, d, s), F32),
        scratch_shapes=[pltpu.VMEM((PEER_HEADS, N_KEYS, tm), F32)] * 4
                       + [pltpu.VMEM((te, tm), F32), pltpu.VMEM((te, tm), BF16)],
        compiler_params=_cparams(("parallel", "parallel", "arbitrary")),
        name="peer_experts",
    )(hn2t, st, stats, u, vt)


def _final_kernel(x1_ref, ft_ref, mod_ref, g_ref, o_ref):
    x2 = x1_ref[0] + mod_ref[0, 5:6, :] * ft_ref[0].T
    y = x2 * lax.rsqrt(jnp.mean(x2 * x2, axis=-1, keepdims=True) + EPS)
    o_ref[0] = y * g_ref[...]


def _final(x1, ft, mod, final_g, tm=512):
    bsz, s, d = x1.shape
    return pl.pallas_call(
        _final_kernel,
        grid=(bsz, s // tm),
        in_specs=[pl.BlockSpec((1, tm, d), lambda b, i: (b, i, 0)),
                  pl.BlockSpec((1, d, tm), lambda b, i: (b, 0, i)),
                  pl.BlockSpec((1, 6, d), lambda b, i: (b, 0, 0)), _const_spec((1, d))],
        out_specs=pl.BlockSpec((1, tm, d), lambda b, i: (b, i, 0)),
        out_shape=jax.ShapeDtypeStruct((bsz, s, d), F32),
        compiler_params=_cparams(("parallel", "parallel")),
        name="final_norm",
    )(x1, ft, mod, final_g.reshape(1, d))


def kernel(x, c, w_ada, b_ada, norm1_g, norm2_g, w_in, rel_bias, w_proj_a, w_proj_b, w_out,
           peer_wq, peer_sub_keys, peer_u, peer_v, final_g):
    bsz, s, d = x.shape
    mod = _ada(c, w_ada[0], b_ada[0]).reshape(bsz, 6, d)
    qa, va, qi, qb, vb, ga, gb, wi, kat, kit, kbt = _in_proj(x, mod, norm1_g[0], w_in[0])
    ya = _dsa(qi, wi, kit, qa, kat, va, _near_bias(rel_bias, DSA_TQ), DSA_TQ)
    yb = _stick_breaking(qb, kbt, vb)
    x1, hn2t, st = _mix(x, ya, yb, ga, gb, mod, norm2_g[0], w_proj_a[0], w_proj_b[0], w_out[0],
                        peer_wq[0], peer_sub_keys[0])
    stats = _peer_select(st)
    ft = _peer_experts(hn2t, st, stats, peer_u[0], peer_v[0])
    return _final(x1, ft, mod, final_g)
```

```python
import functools
import math

import jax
import jax.numpy as jnp
import numpy as np
from jax import lax
from jax.experimental import pallas as pl
from jax.experimental.pallas import tpu as pltpu

N_HEADS = 8
HEAD_DIM = 64
WIDTH = N_HEADS * HEAD_DIM
IDX_DIM = 64
TOPK_MAX = 256
N_BUCKETS = 32
MAX_DISTANCE = 128
EPS = 1e-6

V7X_LANES = 128
V7X_VMEM_LIMIT = 56 * 1024 * 1024

F32 = jnp.float32
BF16 = jnp.bfloat16


def _cparams(sem):
    return pltpu.CompilerParams(dimension_semantics=sem, vmem_limit_bytes=V7X_VMEM_LIMIT)


def _const_spec(shape):
    return pl.BlockSpec(shape, lambda *_: (0,) * len(shape), pipeline_mode=pl.Buffered(1))


def _ada_kernel(c_ref, w_ref, b_ref, o_ref):
    c = c_ref[...]
    cond = c * jax.nn.sigmoid(c)
    o_ref[...] = jnp.dot(cond, w_ref[...], preferred_element_type=F32,
                         precision=lax.Precision.HIGHEST) + b_ref[...]


def _ada(c, w, b):
    bsz, d = c.shape
    n = w.shape[1]
    tn = 1536
    return pl.pallas_call(
        _ada_kernel,
        grid=(n // tn,),
        in_specs=[pl.BlockSpec((bsz, d), lambda j: (0, 0)),
                  pl.BlockSpec((d, tn), lambda j: (0, j)),
                  pl.BlockSpec((1, tn), lambda j: (0, j))],
        out_specs=pl.BlockSpec((bsz, tn), lambda j: (0, j)),
        out_shape=jax.ShapeDtypeStruct((bsz, n), F32),
        compiler_params=_cparams(("arbitrary",)),
        name="ada",
    )(c, w, b.reshape(1, n))


def _modulated_norm(x, g, shift, scale):
    y = x * lax.rsqrt(jnp.mean(x * x, axis=-1, keepdims=True) + EPS)
    return (y * g) * (1.0 + scale) + shift


WI_ROWS = 16


def _in_proj_kernel(x_ref, mod_ref, g_ref, wr_ref, wt_ref,
                    ka_ref, ki_ref, qb_ref, vb_ref, ga_ref, gb_ref,
                    qat_ref, vat_ref, qit_ref, wit_ref, kbt_ref):
    h = _modulated_norm(x_ref[0], g_ref[...], mod_ref[0, 0:1, :], mod_ref[0, 1:2, :]).astype(BF16)
    c0 = 0
    for ref in (ka_ref, ki_ref, qb_ref, vb_ref, ga_ref, gb_ref):
        n = ref.shape[-1]
        for part in range(0, n, WIDTH):
            pw = min(WIDTH, n - part)
            ref[0, :, part:part + pw] = jnp.dot(h, wr_ref[:, c0 + part:c0 + part + pw],
                                               preferred_element_type=F32).astype(ref.dtype)
        c0 += n
    nt = (((1,), (1,)), ((), ()))
    r0 = 0
    for ref in (qat_ref, vat_ref, qit_ref, wit_ref, kbt_ref):
        n = ref.shape[1]
        ref[0] = lax.dot_general(wt_ref[r0:r0 + n, :], h, nt, preferred_element_type=F32).astype(ref.dtype)
        r0 += n


def _in_proj(x, mod, g1, w_in, tm=512):
    bsz, s, d = x.shape
    w = WIDTH
    offs = np.cumsum([0, w, w, w, w, IDX_DIM, N_HEADS, w, w, w, d, d])
    col = lambda i: w_in[:, offs[i]:offs[i + 1]]
    qscale = HEAD_DIM ** -0.5
    w_rows = jnp.concatenate([col(1), col(4), col(4), col(6) * qscale, col(8), col(9), col(10)],
                             axis=1).astype(BF16)
    w_t = jnp.concatenate([col(0) * qscale, col(2), col(3),
                           jnp.pad(col(5), ((0, 0), (0, WI_ROWS - N_HEADS))), col(7)], axis=1).T.astype(BF16)

    row = lambda n, dt: jax.ShapeDtypeStruct((bsz, s, n), dt)
    tr = lambda n, dt: jax.ShapeDtypeStruct((bsz, n, s), dt)
    rspec = lambda n: pl.BlockSpec((1, tm, n), lambda b, i: (b, i, 0))
    tspec = lambda n: pl.BlockSpec((1, n, tm), lambda b, i: (b, 0, i))
    return pl.pallas_call(
        _in_proj_kernel,
        grid=(bsz, s // tm),
        in_specs=[rspec(d),
                  pl.BlockSpec((1, 6, d), lambda b, i: (b, 0, 0)),
                  _const_spec((1, d)),
                  _const_spec(w_rows.shape), _const_spec(w_t.shape)],
        out_specs=[rspec(w), rspec(2 * IDX_DIM), rspec(w), rspec(w), rspec(d), rspec(d),
                   tspec(w), tspec(w), tspec(w), tspec(WI_ROWS), tspec(w)],
        out_shape=[row(w, BF16), row(2 * IDX_DIM, BF16), row(w, BF16), row(w, BF16), row(d, F32), row(d, F32),
                   tr(w, BF16), tr(w, BF16), tr(w, BF16), tr(WI_ROWS, F32), tr(w, BF16)],
        compiler_params=_cparams(("parallel", "parallel")),
        name="in_proj",
    )(x, mod, g1.reshape(1, d), w_rows, w_t)


def _t5_bucket_np(n):
    max_exact = N_BUCKETS // 2
    nf = np.maximum(n, 1).astype(np.float32)
    large = max_exact + (np.log(nf / np.float32(max_exact)) / np.float32(math.log(MAX_DISTANCE / max_exact))
                         * np.float32(N_BUCKETS - max_exact)).astype(np.int32)
    large = np.minimum(large, N_BUCKETS - 1)
    return np.where(n < max_exact, n, large).astype(np.int32)


def _near_bias_kernel(bk_ref, rb_ref, o_ref):
    bk = bk_ref[...]
    for h in range(N_HEADS):
        far = rb_ref[N_BUCKETS - 1, h]
        acc = jnp.zeros(bk.shape, F32)
        for b in range(N_BUCKETS - 1):
            acc = jnp.where(bk == b, rb_ref[b, h] - far, acc)
        o_ref[h] = acc


def _near_bias(rel_bias, tq):
    a = np.arange(2 * tq)[:, None]
    b = np.arange(tq)[None, :]
    n = tq + b - a
    far_from = int(np.argmax(_t5_bucket_np(np.arange(4 * MAX_DISTANCE)) == N_BUCKETS - 1))
    assert far_from <= tq, "distances beyond the near zone must share the last bucket"
    bucket = np.where(n >= 0, _t5_bucket_np(np.maximum(n, 0)), N_BUCKETS - 1).astype(np.int32)
    return pl.pallas_call(
        _near_bias_kernel,
        in_specs=[pl.BlockSpec(memory_space=pltpu.VMEM), pl.BlockSpec(memory_space=pltpu.SMEM)],
        out_specs=pl.BlockSpec(memory_space=pltpu.VMEM),
        out_shape=jax.ShapeDtypeStruct((N_HEADS, 2 * tq, tq), F32),
        name="near_bias",
    )(jnp.asarray(bucket), rel_bias)


INT_MIN = -2 ** 31
NEG = -1e30
IDX_BLOCK = 512
DSA_TQ = 128


def _sortable(x):
    bits = pltpu.bitcast(x, jnp.int32)
    return bits ^ ((bits >> 31) & 0x7FFFFFFF)


def _dsa_kernel(qi_ref, wi_ref, kit_ref, qa_ref, kat_ref, va_ref, nb_ref, o_ref,
                keys_ref, qis_ref, qam_ref, m_ref, l_ref, acc_ref, *, tq, topk):
    i = pl.program_id(1)
    t0 = i * tq
    lanes = V7X_LANES
    lo_half = lax.broadcasted_iota(jnp.int32, (tq, lanes), 1) < HEAD_DIM

    for h in range(N_HEADS):
        p = h // 2
        keep = lo_half if h % 2 == 0 else jnp.logical_not(lo_half)
        qis_ref[h * tq:(h + 1) * tq, :] = jnp.where(
            keep, qi_ref[0, :, p * lanes:(p + 1) * lanes].astype(F32), 0.0).astype(BF16)
        qam_ref[h] = jnp.where(keep, qa_ref[0, :, p * lanes:(p + 1) * lanes].astype(F32), 0.0).astype(BF16)

    sk = IDX_BLOCK
    n_blocks = (t0 + tq + sk - 1) // sk
    row_t = t0 + lax.broadcasted_iota(jnp.int32, (tq, sk), 0)
    col = lax.broadcasted_iota(jnp.int32, (tq, sk), 1)
    w = wi_ref[0]

    def score_body(kb, carry):
        s0 = pl.multiple_of(kb * sk, sk)
        z = jnp.dot(qis_ref[...], kit_ref[0, :, pl.ds(s0, sk)], preferred_element_type=F32)
        sc = w[:, 0:1] * jnp.maximum(z[0:tq], 0.0)
        for h in range(1, N_HEADS):
            sc = sc + w[:, h:h + 1] * jnp.maximum(z[h * tq:(h + 1) * tq], 0.0)
        keys_ref[:, pl.ds(s0, sk)] = jnp.where(col + s0 <= row_t, _sortable(sc), INT_MIN)
        return carry

    lax.fori_loop(0, n_blocks, score_body, 0)

    def bisect_body(it, r):
        bit = lax.shift_left(jnp.int32(1), 31 - it)
        cand = (r | bit) ^ INT_MIN

        def count_body(kb, acc):
            s0 = pl.multiple_of(kb * sk, sk)
            ge = jnp.where(keys_ref[:, pl.ds(s0, sk)] >= cand, 1, 0)
            for j in range(sk // lanes):
                acc = acc + ge[:, j * lanes:(j + 1) * lanes]
            return acc

        acc = lax.fori_loop(0, n_blocks, count_body, jnp.zeros((tq, lanes), jnp.int32))
        cnt = jnp.sum(acc, axis=1, keepdims=True)
        return jnp.where(cnt >= topk, r | bit, r)

    r = lax.fori_loop(0, 32, bisect_body, jnp.zeros((tq, 1), jnp.int32))
    thr = jnp.maximum(r ^ INT_MIN, INT_MIN + 1)

    m_ref[...] = jnp.full(m_ref.shape, NEG, F32)
    l_ref[...] = jnp.zeros(l_ref.shape, F32)
    acc_ref[...] = jnp.zeros(acc_ref.shape, F32)

    def attend(s0, width, bias_cols):
        maskadd = jnp.where(keys_ref[:, pl.ds(s0, width)] >= thr, 0.0, NEG)
        for h in range(N_HEADS):
            p = h // 2
            kt = kat_ref[0, p * lanes:(p + 1) * lanes, pl.ds(s0, width)]
            lg = jnp.dot(qam_ref[h], kt, preferred_element_type=F32) + maskadd
            if bias_cols is not None:
                lg = lg + nb_ref[h, :, bias_cols[0]:bias_cols[1]]
            m_old = m_ref[h]
            m_new = jnp.maximum(m_old, jnp.max(lg, axis=1, keepdims=True))
            alpha = jnp.exp(m_old - m_new)
            pr = jnp.exp(lg - m_new)
            l_ref[h] = alpha * l_ref[h] + jnp.sum(pr, axis=1, keepdims=True)
            v = va_ref[0, pl.ds(s0, width), p * lanes:(p + 1) * lanes]
            acc_ref[h] = alpha * acc_ref[h] + jnp.dot(pr.astype(BF16), v, preferred_element_type=F32)
            m_ref[h] = m_new

    far_tiles = jnp.maximum(t0 // lanes - 1, 0)
    per = sk // lanes
    n_far_big = far_tiles // per

    def far_big(kb, carry):
        attend(pl.multiple_of(kb * sk, sk), sk, None)
        return carry

    lax.fori_loop(0, n_far_big, far_big, 0)

    def far_small(kb, carry):
        attend(pl.multiple_of(kb * lanes, lanes), lanes, None)
        return carry

    lax.fori_loop(n_far_big * per, far_tiles, far_small, 0)

    @pl.when(i > 0)
    def _():
        attend(pl.multiple_of(t0 - lanes, lanes), tq + lanes, (0, tq + lanes))

    @pl.when(i == 0)
    def _():
        attend(0, tq, (lanes, tq + lanes))

    for p in range(N_HEADS // 2):
        even = acc_ref[2 * p] / l_ref[2 * p]
        odd = acc_ref[2 * p + 1] / l_ref[2 * p + 1]
        o_ref[0, :, p * lanes:(p + 1) * lanes] = jnp.where(lo_half, even, odd).astype(o_ref.dtype)


def _dsa(qi, wi, kit, qa, kat, va, near_bias, tq):
    bsz, s, w = qa.shape
    topk = min(TOPK_MAX, s // 4)
    lanes = V7X_LANES
    qspec = lambda n: pl.BlockSpec((1, tq, n), lambda b, i: (b, i, 0))
    resident = lambda shape: pl.BlockSpec((1,) + shape, lambda b, i: (b, 0, 0), pipeline_mode=pl.Buffered(1))
    return pl.pallas_call(
        functools.partial(_dsa_kernel, tq=tq, topk=topk),
        grid=(bsz, s // tq),
        in_specs=[qspec(w), qspec(lanes), resident((2 * IDX_DIM, s)),
                  qspec(w), resident((w, s)), resident((s, w)),
                  _const_spec(near_bias.shape)],
        out_specs=qspec(w),
        out_shape=jax.ShapeDtypeStruct((bsz, s, w), BF16),
        scratch_shapes=[pltpu.VMEM((tq, s), jnp.int32),
                        pltpu.VMEM((N_HEADS * tq, lanes), BF16),
                        pltpu.VMEM((N_HEADS, tq, lanes), BF16),
                        pltpu.VMEM((N_HEADS, tq, 1), F32),
                        pltpu.VMEM((N_HEADS, tq, 1), F32),
                        pltpu.VMEM((N_HEADS, tq, lanes), F32)],
        compiler_params=_cparams(("arbitrary", "arbitrary")),
        name="dsa",
    )(qi, wi, kit, qa, kat, va, near_bias)


DSA_T_TQ = 256
SUBLANES = 8
KEY_BITS = 32
IDX_SCALE = (IDX_DIM ** -0.5) * (N_HEADS ** -0.5)
assert DSA_T_TQ == KEY_BITS * SUBLANES, "one key block must fill the 32 bit positions of a sublane tile"


def _bit_transpose(rows):
    a = list(rows)
    j, m = KEY_BITS // 2, 0x0000FFFF
    while j:
        for k in range(KEY_BITS):
            if k & j:
                continue
            t = (a[k] ^ lax.shift_right_logical(a[k + j], jnp.int32(j))) & m
            a[k] = a[k] ^ t
            a[k + j] = a[k + j] ^ lax.shift_left(t, jnp.int32(j))
        j >>= 1
        m ^= m << j
    return a


def _dsa_t_kernel(qit_ref, wit_ref, ki_ref, qat_ref, ka_ref, vat_ref, nb_ref, o_ref,
                  keys_ref, planes_ref, alive_ref, qim_ref, qam_ref, acc_ref, m_ref, l_ref, mask_ref, lg_ref,
                  *, tq, topk):
    i = pl.program_id(1)
    t0 = i * tq
    lanes = V7X_LANES
    sk = tq
    hd = HEAD_DIM
    idx_bits = (keys_ref.shape[0] - 1).bit_length()

    zeros = jnp.zeros((hd, tq), BF16)
    for h in range(N_HEADS):
        own = slice((h % 2) * hd, (h % 2 + 1) * hd)
        other = slice((1 - h % 2) * hd, (2 - h % 2) * hd)
        qim_ref[own, h * tq:(h + 1) * tq] = qit_ref[0, h * hd:(h + 1) * hd, :]
        qim_ref[other, h * tq:(h + 1) * tq] = zeros
        qam_ref[h, own, :] = qat_ref[0, h * hd:(h + 1) * hd, :]
        qam_ref[h, other, :] = zeros

    n_blocks = i + 1
    key_i = lax.broadcasted_iota(jnp.int32, (sk, tq), 0)
    qry_i = t0 + lax.broadcasted_iota(jnp.int32, (sk, tq), 1)
    w = wit_ref[0]

    def score_body(kb, carry):
        s0 = pl.multiple_of(kb * sk, sk)
        z = jnp.dot(ki_ref[0, pl.ds(s0, sk), :], qim_ref[...], preferred_element_type=F32)
        sc = w[0:1, :] * jnp.maximum(z[:, 0:tq], 0.0)
        for h in range(1, N_HEADS):
            sc = sc + w[h:h + 1, :] * jnp.maximum(z[:, h * tq:(h + 1) * tq], 0.0)
        key = jnp.where(key_i + s0 <= qry_i, _sortable(sc * IDX_SCALE), INT_MIN)
        keys_ref[pl.ds(s0, sk), :] = key
        ordered = key ^ INT_MIN
        planes = _bit_transpose([ordered[g * SUBLANES:(g + 1) * SUBLANES, :] for g in range(KEY_BITS)])
        for p in range(KEY_BITS):
            planes_ref[p, kb] = planes[p]
        return carry

    lax.fori_loop(0, n_blocks, score_body, 0)

    def first_count(kb, cnt):
        alive_ref[kb] = jnp.full((SUBLANES, tq), -1, jnp.int32)
        return cnt + lax.population_count(planes_ref[0, kb])

    zero_cnt = jnp.zeros((SUBLANES, tq), jnp.int32)
    cnt0 = lax.fori_loop(0, n_blocks, first_count, zero_cnt)

    def bit_body(p, carry):
        r, k_left, cnt = carry
        total = jnp.sum(cnt, axis=0, keepdims=True)
        take = total >= k_left
        r = jnp.where(take, r | lax.shift_left(jnp.int32(1), KEY_BITS - 1 - p), r)
        k_left = jnp.where(take, k_left, k_left - total)
        drop = jnp.where(take, 0, -1)
        p_next = jnp.minimum(p + 1, KEY_BITS - 1)

        def narrow(kb, c):
            alive = alive_ref[kb] & (planes_ref[p, kb] ^ drop)
            alive_ref[kb] = alive
            return c + lax.population_count(alive & planes_ref[p_next, kb])

        return r, k_left, lax.fori_loop(0, n_blocks, narrow, zero_cnt)

    r, k_left, _ = lax.fori_loop(0, KEY_BITS, bit_body,
                                 (jnp.zeros((1, tq), jnp.int32), jnp.full((1, tq), topk, jnp.int32), cnt0))
    thr = jnp.maximum(r ^ INT_MIN, INT_MIN + 1)

    def count_alive(kb, c):
        return c + lax.population_count(alive_ref[kb])

    tied = jnp.sum(lax.fori_loop(0, n_blocks, count_alive, zero_cnt), axis=0, keepdims=True)
    surplus = jnp.where(jnp.logical_and(tied > k_left, r != 0), 1, 0)

    @pl.when(jnp.max(surplus) > 0)
    def _():
        def tied_before(cand):
            def body(kb, c):
                s0 = pl.multiple_of(kb * sk, sk)
                hit = jnp.where(keys_ref[pl.ds(s0, sk), :] == thr, jnp.where(key_i + s0 < cand, 1, 0), 0)
                for g in range(sk // SUBLANES):
                    c = c + hit[g * SUBLANES:(g + 1) * SUBLANES, :]
                return c
            return jnp.sum(lax.fori_loop(0, n_blocks, body, zero_cnt), axis=0, keepdims=True)

        def index_bit(it, c):
            cand = c | lax.shift_left(jnp.int32(1), idx_bits - 1 - it)
            return jnp.where(tied_before(cand) < k_left, cand, c)

        last_kept = lax.fori_loop(0, idx_bits, index_bit, jnp.zeros((1, tq), jnp.int32))

        def demote(kb, carry):
            s0 = pl.multiple_of(kb * sk, sk)
            key = keys_ref[pl.ds(s0, sk), :]
            drop = jnp.where(key == thr, jnp.where(key_i + s0 > last_kept, 1, 0), 0)
            keys_ref[pl.ds(s0, sk), :] = jnp.where(drop > 0, thr - 1, key)
            return carry

        lax.fori_loop(0, n_blocks, demote, 0)

    m_ref[...] = jnp.full(m_ref.shape, NEG, F32)
    l_ref[...] = jnp.zeros(l_ref.shape, F32)
    acc_ref[...] = jnp.zeros(acc_ref.shape, F32)

    def attend(s0, bias_row0):
        mask_ref[...] = jnp.where(keys_ref[pl.ds(s0, sk), :] >= thr, 0.0, NEG)

        def logits(h):
            p = h // 2
            lg = jnp.dot(ka_ref[0, pl.ds(s0, sk), p * lanes:(p + 1) * lanes], qam_ref[h],
                         preferred_element_type=F32) + mask_ref[...]
            if bias_row0 is not None:
                lg = lg + nb_ref[h, bias_row0:bias_row0 + sk, :]
            lg_ref[h] = lg
            return jnp.max(lg, axis=0, keepdims=True)

        def accumulate(h, block_max):
            hs = slice(h * hd, (h + 1) * hd)
            m_old = m_ref[h, 0:1, :]
            m_new = jnp.maximum(m_old, block_max)
            alpha = jnp.exp(m_old - m_new)
            pr = jnp.exp(lg_ref[h] - m_new)
            l_ref[h, 0:1, :] = alpha * l_ref[h, 0:1, :] + jnp.sum(pr, axis=0, keepdims=True)
            acc_ref[hs, :] = alpha * acc_ref[hs, :] + jnp.dot(
                vat_ref[0, hs, pl.ds(s0, sk)], pr.astype(BF16), preferred_element_type=F32)
            m_ref[h, 0:1, :] = m_new

        block_max = logits(0)
        for h in range(1, N_HEADS):
            next_max = logits(h)
            accumulate(h - 1, block_max)
            block_max = next_max
        accumulate(N_HEADS - 1, block_max)

    def far_body(kb, carry):
        attend(pl.multiple_of(kb * sk, sk), None)
        return carry

    lax.fori_loop(0, i - 1, far_body, 0)

    @pl.when(i > 0)
    def _():
        attend(pl.multiple_of(t0 - sk, sk), 0)

    attend(pl.multiple_of(t0, sk), sk)

    for h in range(N_HEADS):
        hs = slice(h * hd, (h + 1) * hd)
        acc_ref[hs, :] = acc_ref[hs, :] / l_ref[h, 0:1, :]
    o_ref[0] = acc_ref[...].T.astype(o_ref.dtype)


def _dsa_t(qit, wit, ki, qat, ka, vat, near_bias, tq):
    bsz, w, s = qat.shape
    topk = min(TOPK_MAX, s // 4)
    tspec = lambda n: pl.BlockSpec((1, n, tq), lambda b, i: (b, 0, i))
    resident = lambda shape: pl.BlockSpec((1,) + shape, lambda b, i: (b, 0, 0), pipeline_mode=pl.Buffered(1))
    return pl.pallas_call(
        functools.partial(_dsa_t_kernel, tq=tq, topk=topk),
        grid=(bsz, s // tq),
        in_specs=[tspec(w), tspec(WI_ROWS), resident((s, 2 * IDX_DIM)),
                  tspec(w), resident((s, w)), resident((w, s)),
                  _const_spec(near_bias.shape)],
        out_specs=pl.BlockSpec((1, tq, w), lambda b, i: (b, i, 0)),
        out_shape=jax.ShapeDtypeStruct((bsz, s, w), BF16),
        scratch_shapes=[pltpu.VMEM((s, tq), jnp.int32),
                        pltpu.VMEM((KEY_BITS, s // tq, SUBLANES, tq), jnp.int32),
                        pltpu.VMEM((s // tq, SUBLANES, tq), jnp.int32),
                        pltpu.VMEM((2 * HEAD_DIM, N_HEADS * tq), BF16),
                        pltpu.VMEM((N_HEADS, 2 * HEAD_DIM, tq), BF16),
                        pltpu.VMEM((w, tq), F32),
                        pltpu.VMEM((N_HEADS, SUBLANES, tq), F32),
                        pltpu.VMEM((N_HEADS, SUBLANES, tq), F32),
                        pltpu.VMEM((tq, tq), F32),
                        pltpu.VMEM((N_HEADS, tq, tq), F32)],
        compiler_params=_cparams(("arbitrary", "arbitrary")),
        name="dsa",
    )(qit, wit, ki, qat, ka, vat, near_bias)


SB_TQ = 256
SB_DEAD = -104.0


def _sb_kernel(q_ref, kt_ref, v_ref, o_ref, *, tq):
    i = pl.program_id(2)
    lanes = V7X_LANES
    lo_half = lax.broadcasted_iota(jnp.int32, (tq, lanes), 1) < HEAD_DIM
    r_i = lax.broadcasted_iota(jnp.int32, (tq, tq), 0)
    c_i = lax.broadcasted_iota(jnp.int32, (tq, tq), 1)
    strict = c_i < r_i
    after = jnp.where(r_i > c_i, 1.0, 0.0).astype(BF16)
    after2 = jnp.concatenate([after, after], axis=0)
    q = q_ref[0].astype(F32)
    qms = [jnp.where(lo_half, q, 0.0).astype(BF16), jnp.where(lo_half, 0.0, q).astype(BF16)]

    def tile(kb, diag, state):
        s0 = pl.multiple_of(kb * tq, tq)
        kt = kt_ref[0, :, pl.ds(s0, tq)]
        v = v_ref[0, pl.ds(s0, tq), :]
        carries, accs = [], []
        for hh in range(2):
            carry, acc = state[hh], state[2 + hh]
            z = jnp.dot(qms[hh], kt, preferred_element_type=F32)
            sp = jnp.log(1.0 + jnp.exp(-jnp.abs(z)))
            log_beta = jnp.minimum(z, 0.0) - sp
            log_rest = -(jnp.maximum(z, 0.0) + sp)
            if diag:
                log_rest = jnp.where(strict, log_rest, 0.0)
            hi = log_rest.astype(BF16)
            lo = (log_rest - hi.astype(F32)).astype(BF16)
            suffix = jnp.dot(jnp.concatenate([hi, lo], axis=1), after2, preferred_element_type=F32)
            a = jnp.exp(log_beta + suffix + carry)
            if diag:
                a = jnp.where(strict, a, 0.0)
            accs.append(acc + jnp.dot(a.astype(BF16), v, preferred_element_type=F32))
            carries.append(carry + jnp.sum(log_rest, axis=1, keepdims=True))
        return tuple(carries) + tuple(accs)

    zc = jnp.zeros((tq, 1), F32)
    za = jnp.zeros((tq, lanes), F32)
    state = tile(i, True, (zc, zc, za, za))

    def alive(ls):
        kb, state = ls
        worst = jnp.max(jnp.maximum(state[0], state[1]))
        return jnp.logical_and(kb >= 0, worst >= SB_DEAD)

    def body(ls):
        kb, state = ls
        return kb - 1, tile(kb, False, state)

    _, state = lax.while_loop(alive, body, (i - 1, state))
    o_ref[0] = jnp.where(lo_half, state[2], state[3]).astype(o_ref.dtype)


def _stick_breaking(qb, kbt, vb, tq=SB_TQ):
    bsz, s, w = qb.shape
    lanes = V7X_LANES
    return pl.pallas_call(
        functools.partial(_sb_kernel, tq=tq),
        grid=(bsz, w // lanes, s // tq),
        in_specs=[pl.BlockSpec((1, tq, lanes), lambda b, p, i: (b, i, p)),
                  pl.BlockSpec((1, lanes, s), lambda b, p, i: (b, p, 0)),
                  pl.BlockSpec((1, s, lanes), lambda b, p, i: (b, 0, p))],
        out_specs=pl.BlockSpec((1, tq, lanes), lambda b, p, i: (b, i, p)),
        out_shape=jax.ShapeDtypeStruct((bsz, s, w), BF16),
        compiler_params=_cparams(("arbitrary", "arbitrary", "arbitrary")),
        name="stick_breaking",
    )(qb, kbt, vb)


PEER_HEADS = 8
N_KEYS = 128
PEER_HALF = 128
PEER_TOPK = 16
PEER_GROUPS = 2 * PEER_HEADS


def _mix_kernel(x_ref, ya_ref, yb_ref, ga_ref, gb_ref, mod_ref, g2_ref, wa_ref, wb_ref, wo_ref, wq_ref,
                sk_ref, x1_ref, hn2t_ref, st_ref):
    ma = jnp.dot(ya_ref[0], wa_ref[...], preferred_element_type=F32)
    mb = jnp.dot(yb_ref[0], wb_ref[...], preferred_element_type=F32)
    merged = jax.nn.sigmoid(ga_ref[0]) * ma + jax.nn.sigmoid(gb_ref[0]) * mb
    x1 = x_ref[0] + mod_ref[0, 2:3, :] * jnp.dot(merged.astype(BF16), wo_ref[...], preferred_element_type=F32)
    x1_ref[0] = x1
    h2 = _modulated_norm(x1, g2_ref[...], mod_ref[0, 3:4, :], mod_ref[0, 4:5, :])
    hn2t_ref[0] = h2.T.astype(BF16)
    qh = jnp.dot(h2.astype(BF16), wq_ref[...], preferred_element_type=F32).astype(BF16)
    nt = (((1,), (1,)), ((), ()))
    for g in range(PEER_GROUPS):
        st_ref[0, g * N_KEYS:(g + 1) * N_KEYS, :] = lax.dot_general(
            sk_ref[g], qh[:, g * PEER_HALF:(g + 1) * PEER_HALF], nt, preferred_element_type=F32)


def _mix(x, ya, yb, ga, gb, mod, g2, w_proj_a, w_proj_b, w_out, peer_wq, sub_keys, tm=512):
    bsz, s, d = x.shape
    w = ya.shape[-1]
    nq = peer_wq.shape[1]
    sk = sub_keys.reshape(PEER_GROUPS, N_KEYS, PEER_HALF).astype(BF16)
    rspec = lambda n: pl.BlockSpec((1, tm, n), lambda b, i: (b, i, 0))
    tspec = lambda n: pl.BlockSpec((1, n, tm), lambda b, i: (b, 0, i))
    return pl.pallas_call(
        _mix_kernel,
        grid=(bsz, s // tm),
        in_specs=[rspec(d), rspec(w), rspec(w), rspec(d), rspec(d),
                  pl.BlockSpec((1, 6, d), lambda b, i: (b, 0, 0)), _const_spec((1, d)),
                  _const_spec((w, d)), _const_spec((w, d)), _const_spec((d, d)), _const_spec((d, nq)),
                  _const_spec(sk.shape)],
        out_specs=[rspec(d), tspec(d), tspec(PEER_GROUPS * N_KEYS)],
        out_shape=[jax.ShapeDtypeStruct((bsz, s, d), F32),
                   jax.ShapeDtypeStruct((bsz, d, s), BF16),
                   jax.ShapeDtypeStruct((bsz, PEER_GROUPS * N_KEYS, s), F32)],
        compiler_params=_cparams(("parallel", "parallel")),
        name="mix_out",
    )(x, ya, yb, ga, gb, mod, g2.reshape(1, d), w_proj_a.astype(BF16), w_proj_b.astype(BF16),
      w_out.astype(BF16), peer_wq.astype(BF16), sk)


N_STATS = 8


def _peer_select_kernel(st_ref, stats_ref, top_ref, cand_ref):
    tn = st_ref.shape[-1]

    def group_body(g, carry):
        sc = st_ref[0, pl.ds(pl.multiple_of(g * N_KEYS, N_KEYS), N_KEYS), :]
        for k in range(PEER_TOPK):
            m = jnp.max(sc, axis=0, keepdims=True)
            top_ref[g, k:k + 1, :] = m
            sc = jnp.where(sc == m, -jnp.inf, sc)
        return carry

    lax.fori_loop(0, PEER_GROUPS, group_body, 0)

    def head_body(h, carry):
        v1 = top_ref[2 * h]
        v2 = top_ref[2 * h + 1]
        for k in range(PEER_TOPK):
            cand_ref[k * PEER_TOPK:(k + 1) * PEER_TOPK, :] = v1[k:k + 1, :] + v2
        cand = cand_ref[...]
        m = None
        sc = cand
        for k in range(PEER_TOPK):
            mk = jnp.max(sc, axis=0, keepdims=True)
            m = mk if m is None else m
            sc = jnp.where(sc == mk, -jnp.inf, sc)
        thr = mk
        z = jnp.sum(jnp.where(cand >= thr, jnp.exp(cand - m), 0.0), axis=0, keepdims=True)
        rows = [v1[PEER_TOPK - 1:PEER_TOPK, :], v2[PEER_TOPK - 1:PEER_TOPK, :], thr,
                v1[0:1, :], v2[0:1, :], 1.0 / z, jnp.zeros((2, tn), F32)]
        stats_ref[0, pl.ds(pl.multiple_of(h * N_STATS, N_STATS), N_STATS), :] = jnp.concatenate(rows, axis=0)
        return carry

    lax.fori_loop(0, PEER_HEADS, head_body, 0)


def _peer_select(st, tn=512):
    bsz, rows, s = st.shape
    return pl.pallas_call(
        _peer_select_kernel,
        grid=(bsz, s // tn),
        in_specs=[pl.BlockSpec((1, rows, tn), lambda b, i: (b, 0, i))],
        out_specs=pl.BlockSpec((1, PEER_HEADS * N_STATS, tn), lambda b, i: (b, 0, i)),
        out_shape=jax.ShapeDtypeStruct((bsz, PEER_HEADS * N_STATS, s), F32),
        scratch_shapes=[pltpu.VMEM((PEER_GROUPS, PEER_TOPK, tn), F32),
                        pltpu.VMEM((PEER_TOPK * PEER_TOPK, tn), F32)],
        compiler_params=_cparams(("parallel", "parallel")),
        name="peer_select",
    )(st)


def _peer_experts_kernel(ht_ref, st_ref, stats_ref, u_ref, vt_ref, o_ref, a1_ref, e1_ref, a2_ref, e2_ref,
                         pre_ref, coef_ref, *, te, n_chunks):
    j = pl.program_id(2)
    tm = ht_ref.shape[-1]
    lanes = V7X_LANES
    rows_per_step = te // N_KEYS

    @pl.when(j == 0)
    def _():
        o_ref[...] = jnp.zeros(o_ref.shape, F32)
        coef_ref[1] = jnp.zeros(coef_ref.shape[1:], BF16)
        for h in range(PEER_HEADS):
            st = lambda k: stats_ref[0, h * N_STATS + k:h * N_STATS + k + 1, :]
            s1 = st_ref[0, (2 * h) * N_KEYS:(2 * h + 1) * N_KEYS, :]
            s2 = st_ref[0, (2 * h + 1) * N_KEYS:(2 * h + 2) * N_KEYS, :]
            in1 = s1 >= st(0)
            in2 = s2 >= st(1)
            a1_ref[h] = jnp.where(in1, s1, NEG)
            e1_ref[h] = jnp.where(in1, jnp.exp(s1 - st(3)) * st(5), 0.0)
            a2_ref[h] = jnp.where(in2, s2, NEG)
            e2_ref[h] = jnp.where(in2, jnp.exp(s2 - st(4)), 0.0)

    def add_previous_chunk():
        o_ref[0] += jnp.dot(vt_ref[...], coef_ref[(j + 1) % 2], preferred_element_type=F32)

    @pl.when(j < n_chunks)
    def _():
        ht = ht_ref[0]
        for r in range(rows_per_step):
            rs = slice(r * N_KEYS, (r + 1) * N_KEYS)
            pre_ref[rs, :] = jnp.dot(u_ref[rs, :], ht, preferred_element_type=F32)
        add_previous_chunk()
        first = pl.multiple_of(j * rows_per_step, rows_per_step)
        slot = j % 2
        for c in range(tm // lanes):
            cs = slice(c * lanes, (c + 1) * lanes)
            for r in range(rows_per_step):
                rs = slice(r * N_KEYS, (r + 1) * N_KEYS)
                gate = jnp.zeros((N_KEYS, lanes), F32)
                for h in range(PEER_HEADS):
                    ssum = a1_ref[h, pl.ds(first, rows_per_step), cs][r:r + 1, :] + a2_ref[h, :, cs]
                    wgt = e1_ref[h, pl.ds(first, rows_per_step), cs][r:r + 1, :] * e2_ref[h, :, cs]
                    thr = stats_ref[0, h * N_STATS + 2:h * N_STATS + 3, cs]
                    gate = gate + jnp.where(ssum >= thr, wgt, 0.0)
                x = pre_ref[rs, cs]
                gelu = 0.5 * x * (1.0 + lax.erf(x * np.float32(math.sqrt(0.5))))
                coef_ref[slot, rs, cs] = (gate * gelu).astype(BF16)

    @pl.when(j == n_chunks)
    def _():
        add_previous_chunk()


def _peer_experts(hn2t, st, stats, peer_u, peer_v, tm=512, te=1024):
    bsz, d, s = hn2t.shape
    n_exp = peer_u.shape[0]
    n_chunks = n_exp // te
    u = peer_u.astype(BF16)
    vt = peer_v.T.astype(BF16)
    tok = lambda n: pl.BlockSpec((1, n, tm), lambda b, i, j: (b, 0, i))
    return pl.pallas_call(
        functools.partial(_peer_experts_kernel, te=te, n_chunks=n_chunks),
        grid=(bsz, s // tm, n_chunks + 1),
        in_specs=[tok(d), tok(st.shape[1]), tok(stats.shape[1]),
                  pl.BlockSpec((te, d), lambda b, i, j: (jnp.minimum(j, n_chunks - 1), 0)),
                  pl.BlockSpec((d, te), lambda b, i, j: (0, jnp.maximum(j - 1, 0)))],
        out_specs=tok(d),
        out_shape=jax.ShapeDtypeStruct((bsz, d, s), F32),
        scratch_shapes=[pltpu.VMEM((PEER_HEADS, N_KEYS, tm), F32)] * 4
                       + [pltpu.VMEM((te, tm), F32), pltpu.VMEM((2, te, tm), BF16)],
        compiler_params=_cparams(("parallel", "parallel", "arbitrary")),
        name="peer_experts",
    )(hn2t, st, stats, u, vt)


def _final_kernel(x1_ref, ft_ref, mod_ref, g_ref, o_ref):
    x2 = x1_ref[0] + mod_ref[0, 5:6, :] * ft_ref[0].T
    y = x2 * lax.rsqrt(jnp.mean(x2 * x2, axis=-1, keepdims=True) + EPS)
    o_ref[0] = y * g_ref[...]


def _final(x1, ft, mod, final_g, tm=512):
    bsz, s, d = x1.shape
    return pl.pallas_call(
        _final_kernel,
        grid=(bsz, s // tm),
        in_specs=[pl.BlockSpec((1, tm, d), lambda b, i: (b, i, 0)),
                  pl.BlockSpec((1, d, tm), lambda b, i: (b, 0, i)),
                  pl.BlockSpec((1, 6, d), lambda b, i: (b, 0, 0)), _const_spec((1, d))],
        out_specs=pl.BlockSpec((1, tm, d), lambda b, i: (b, i, 0)),
        out_shape=jax.ShapeDtypeStruct((bsz, s, d), F32),
        compiler_params=_cparams(("parallel", "parallel")),
        name="final_norm",
    )(x1, ft, mod, final_g.reshape(1, d))


def kernel(x, c, w_ada, b_ada, norm1_g, norm2_g, w_in, rel_bias, w_proj_a, w_proj_b, w_out,
           peer_wq, peer_sub_keys, peer_u, peer_v, final_g):
    bsz, s, d = x.shape
    mod = _ada(c, w_ada[0], b_ada[0]).reshape(bsz, 6, d)
    ka, ki, qb, vb, ga, gb, qat, vat, qit, wit, kbt = _in_proj(x, mod, norm1_g[0], w_in[0])
    ya = _dsa_t(qit, wit, ki, qat, ka, vat, _near_bias(rel_bias, DSA_T_TQ), DSA_T_TQ)
    yb = _stick_breaking(qb, kbt, vb)
    x1, hn2t, st = _mix(x, ya, yb, ga, gb, mod, norm2_g[0], w_proj_a[0], w_proj_b[0], w_out[0],
                        peer_wq[0], peer_sub_keys[0])
    stats = _peer_select(st)
    ft = _peer_experts(hn2t, st, stats, peer_u[0], peer_v[0])
    return _final(x1, ft, mod, final_g)
```

```python
import functools
import math

import jax
import jax.numpy as jnp
import numpy as np
from jax import lax
from jax.experimental import pallas as pl
from jax.experimental.pallas import tpu as pltpu

N_HEADS = 8
HEAD_DIM = 64
WIDTH = N_HEADS * HEAD_DIM
IDX_DIM = 64
TOPK_MAX = 256
N_BUCKETS = 32
MAX_DISTANCE = 128
EPS = 1e-6

V7X_LANES = 128
V7X_VMEM_LIMIT = 56 * 1024 * 1024

F32 = jnp.float32
BF16 = jnp.bfloat16


def _cparams(sem):
    return pltpu.CompilerParams(dimension_semantics=sem, vmem_limit_bytes=V7X_VMEM_LIMIT)


def _const_spec(shape):
    return pl.BlockSpec(shape, lambda *_: (0,) * len(shape), pipeline_mode=pl.Buffered(1))


def _ada_kernel(c_ref, w_ref, b_ref, o_ref):
    c = c_ref[...]
    cond = c * jax.nn.sigmoid(c)
    o_ref[...] = jnp.dot(cond, w_ref[...], preferred_element_type=F32,
                         precision=lax.Precision.HIGHEST) + b_ref[...]


def _ada(c, w, b):
    bsz, d = c.shape
    n = w.shape[1]
    tn = 1536
    return pl.pallas_call(
        _ada_kernel,
        grid=(n // tn,),
        in_specs=[pl.BlockSpec((bsz, d), lambda j: (0, 0)),
                  pl.BlockSpec((d, tn), lambda j: (0, j)),
                  pl.BlockSpec((1, tn), lambda j: (0, j))],
        out_specs=pl.BlockSpec((bsz, tn), lambda j: (0, j)),
        out_shape=jax.ShapeDtypeStruct((bsz, n), F32),
        compiler_params=_cparams(("arbitrary",)),
        name="ada",
    )(c, w, b.reshape(1, n))


def _modulated_norm(x, g, shift, scale):
    y = x * lax.rsqrt(jnp.mean(x * x, axis=-1, keepdims=True) + EPS)
    return (y * g) * (1.0 + scale) + shift


WI_ROWS = 16


def _in_proj_kernel(x_ref, mod_ref, g_ref, wr_ref, wt_ref,
                    ka_ref, ki_ref, qb_ref, vb_ref, ga_ref, gb_ref,
                    qat_ref, vat_ref, qit_ref, wit_ref, kbt_ref):
    h = _modulated_norm(x_ref[0], g_ref[...], mod_ref[0, 0:1, :], mod_ref[0, 1:2, :]).astype(BF16)
    c0 = 0
    for ref in (ka_ref, ki_ref, qb_ref, vb_ref, ga_ref, gb_ref):
        n = ref.shape[-1]
        for part in range(0, n, WIDTH):
            pw = min(WIDTH, n - part)
            ref[0, :, part:part + pw] = jnp.dot(h, wr_ref[:, c0 + part:c0 + part + pw],
                                               preferred_element_type=F32).astype(ref.dtype)
        c0 += n
    nt = (((1,), (1,)), ((), ()))
    r0 = 0
    for ref in (qat_ref, vat_ref, qit_ref, wit_ref, kbt_ref):
        n = ref.shape[1]
        ref[0] = lax.dot_general(wt_ref[r0:r0 + n, :], h, nt, preferred_element_type=F32).astype(ref.dtype)
        r0 += n


def _in_proj(x, mod, g1, w_in, tm=512):
    bsz, s, d = x.shape
    w = WIDTH
    offs = np.cumsum([0, w, w, w, w, IDX_DIM, N_HEADS, w, w, w, d, d])
    col = lambda i: w_in[:, offs[i]:offs[i + 1]]
    qscale = HEAD_DIM ** -0.5
    w_rows = jnp.concatenate([col(1), col(4), col(4), col(6) * qscale, col(8), col(9), col(10)],
                             axis=1).astype(BF16)
    w_t = jnp.concatenate([col(0) * qscale, col(2), col(3),
                           jnp.pad(col(5), ((0, 0), (0, WI_ROWS - N_HEADS))), col(7)], axis=1).T.astype(BF16)

    row = lambda n, dt: jax.ShapeDtypeStruct((bsz, s, n), dt)
    tr = lambda n, dt: jax.ShapeDtypeStruct((bsz, n, s), dt)
    rspec = lambda n: pl.BlockSpec((1, tm, n), lambda b, i: (b, i, 0))
    tspec = lambda n: pl.BlockSpec((1, n, tm), lambda b, i: (b, 0, i))
    return pl.pallas_call(
        _in_proj_kernel,
        grid=(bsz, s // tm),
        in_specs=[rspec(d),
                  pl.BlockSpec((1, 6, d), lambda b, i: (b, 0, 0)),
                  _const_spec((1, d)),
                  _const_spec(w_rows.shape), _const_spec(w_t.shape)],
        out_specs=[rspec(w), rspec(2 * IDX_DIM), rspec(w), rspec(w), rspec(d), rspec(d),
                   tspec(w), tspec(w), tspec(w), tspec(WI_ROWS), tspec(w)],
        out_shape=[row(w, BF16), row(2 * IDX_DIM, BF16), row(w, BF16), row(w, BF16), row(d, F32), row(d, F32),
                   tr(w, BF16), tr(w, BF16), tr(w, BF16), tr(WI_ROWS, F32), tr(w, BF16)],
        compiler_params=_cparams(("parallel", "parallel")),
        name="in_proj",
    )(x, mod, g1.reshape(1, d), w_rows, w_t)


def _t5_bucket_np(n):
    max_exact = N_BUCKETS // 2
    nf = np.maximum(n, 1).astype(np.float32)
    large = max_exact + (np.log(nf / np.float32(max_exact)) / np.float32(math.log(MAX_DISTANCE / max_exact))
                         * np.float32(N_BUCKETS - max_exact)).astype(np.int32)
    large = np.minimum(large, N_BUCKETS - 1)
    return np.where(n < max_exact, n, large).astype(np.int32)


def _near_bias_kernel(bk_ref, rb_ref, o_ref):
    bk = bk_ref[...]
    for h in range(N_HEADS):
        far = rb_ref[N_BUCKETS - 1, h]
        acc = jnp.zeros(bk.shape, F32)
        for b in range(N_BUCKETS - 1):
            acc = jnp.where(bk == b, rb_ref[b, h] - far, acc)
        o_ref[h] = acc


def _near_bias(rel_bias, tq):
    a = np.arange(2 * tq)[:, None]
    b = np.arange(tq)[None, :]
    n = tq + b - a
    far_from = int(np.argmax(_t5_bucket_np(np.arange(4 * MAX_DISTANCE)) == N_BUCKETS - 1))
    assert far_from <= tq, "distances beyond the near zone must share the last bucket"
    bucket = np.where(n >= 0, _t5_bucket_np(np.maximum(n, 0)), N_BUCKETS - 1).astype(np.int32)
    return pl.pallas_call(
        _near_bias_kernel,
        in_specs=[pl.BlockSpec(memory_space=pltpu.VMEM), pl.BlockSpec(memory_space=pltpu.SMEM)],
        out_specs=pl.BlockSpec(memory_space=pltpu.VMEM),
        out_shape=jax.ShapeDtypeStruct((N_HEADS, 2 * tq, tq), F32),
        name="near_bias",
    )(jnp.asarray(bucket), rel_bias)


INT_MIN = -2 ** 31
NEG = -1e30
IDX_BLOCK = 512
DSA_TQ = 128


def _sortable(x):
    bits = pltpu.bitcast(x, jnp.int32)
    return bits ^ ((bits >> 31) & 0x7FFFFFFF)


def _dsa_kernel(qi_ref, wi_ref, kit_ref, qa_ref, kat_ref, va_ref, nb_ref, o_ref,
                keys_ref, qis_ref, qam_ref, m_ref, l_ref, acc_ref, *, tq, topk):
    i = pl.program_id(1)
    t0 = i * tq
    lanes = V7X_LANES
    lo_half = lax.broadcasted_iota(jnp.int32, (tq, lanes), 1) < HEAD_DIM

    for h in range(N_HEADS):
        p = h // 2
        keep = lo_half if h % 2 == 0 else jnp.logical_not(lo_half)
        qis_ref[h * tq:(h + 1) * tq, :] = jnp.where(
            keep, qi_ref[0, :, p * lanes:(p + 1) * lanes].astype(F32), 0.0).astype(BF16)
        qam_ref[h] = jnp.where(keep, qa_ref[0, :, p * lanes:(p + 1) * lanes].astype(F32), 0.0).astype(BF16)

    sk = IDX_BLOCK
    n_blocks = (t0 + tq + sk - 1) // sk
    row_t = t0 + lax.broadcasted_iota(jnp.int32, (tq, sk), 0)
    col = lax.broadcasted_iota(jnp.int32, (tq, sk), 1)
    w = wi_ref[0]

    def score_body(kb, carry):
        s0 = pl.multiple_of(kb * sk, sk)
        z = jnp.dot(qis_ref[...], kit_ref[0, :, pl.ds(s0, sk)], preferred_element_type=F32)
        sc = w[:, 0:1] * jnp.maximum(z[0:tq], 0.0)
        for h in range(1, N_HEADS):
            sc = sc + w[:, h:h + 1] * jnp.maximum(z[h * tq:(h + 1) * tq], 0.0)
        keys_ref[:, pl.ds(s0, sk)] = jnp.where(col + s0 <= row_t, _sortable(sc), INT_MIN)
        return carry

    lax.fori_loop(0, n_blocks, score_body, 0)

    def bisect_body(it, r):
        bit = lax.shift_left(jnp.int32(1), 31 - it)
        cand = (r | bit) ^ INT_MIN

        def count_body(kb, acc):
            s0 = pl.multiple_of(kb * sk, sk)
            ge = jnp.where(keys_ref[:, pl.ds(s0, sk)] >= cand, 1, 0)
            for j in range(sk // lanes):
                acc = acc + ge[:, j * lanes:(j + 1) * lanes]
            return acc

        acc = lax.fori_loop(0, n_blocks, count_body, jnp.zeros((tq, lanes), jnp.int32))
        cnt = jnp.sum(acc, axis=1, keepdims=True)
        return jnp.where(cnt >= topk, r | bit, r)

    r = lax.fori_loop(0, 32, bisect_body, jnp.zeros((tq, 1), jnp.int32))
    thr = jnp.maximum(r ^ INT_MIN, INT_MIN + 1)

    m_ref[...] = jnp.full(m_ref.shape, NEG, F32)
    l_ref[...] = jnp.zeros(l_ref.shape, F32)
    acc_ref[...] = jnp.zeros(acc_ref.shape, F32)

    def attend(s0, width, bias_cols):
        maskadd = jnp.where(keys_ref[:, pl.ds(s0, width)] >= thr, 0.0, NEG)
        for h in range(N_HEADS):
            p = h // 2
            kt = kat_ref[0, p * lanes:(p + 1) * lanes, pl.ds(s0, width)]
            lg = jnp.dot(qam_ref[h], kt, preferred_element_type=F32) + maskadd
            if bias_cols is not None:
                lg = lg + nb_ref[h, :, bias_cols[0]:bias_cols[1]]
            m_old = m_ref[h]
            m_new = jnp.maximum(m_old, jnp.max(lg, axis=1, keepdims=True))
            alpha = jnp.exp(m_old - m_new)
            pr = jnp.exp(lg - m_new)
            l_ref[h] = alpha * l_ref[h] + jnp.sum(pr, axis=1, keepdims=True)
            v = va_ref[0, pl.ds(s0, width), p * lanes:(p + 1) * lanes]
            acc_ref[h] = alpha * acc_ref[h] + jnp.dot(pr.astype(BF16), v, preferred_element_type=F32)
            m_ref[h] = m_new

    far_tiles = jnp.maximum(t0 // lanes - 1, 0)
    per = sk // lanes
    n_far_big = far_tiles // per

    def far_big(kb, carry):
        attend(pl.multiple_of(kb * sk, sk), sk, None)
        return carry

    lax.fori_loop(0, n_far_big, far_big, 0)

    def far_small(kb, carry):
        attend(pl.multiple_of(kb * lanes, lanes), lanes, None)
        return carry

    lax.fori_loop(n_far_big * per, far_tiles, far_small, 0)

    @pl.when(i > 0)
    def _():
        attend(pl.multiple_of(t0 - lanes, lanes), tq + lanes, (0, tq + lanes))

    @pl.when(i == 0)
    def _():
        attend(0, tq, (lanes, tq + lanes))

    for p in range(N_HEADS // 2):
        even = acc_ref[2 * p] / l_ref[2 * p]
        odd = acc_ref[2 * p + 1] / l_ref[2 * p + 1]
        o_ref[0, :, p * lanes:(p + 1) * lanes] = jnp.where(lo_half, even, odd).astype(o_ref.dtype)


def _dsa(qi, wi, kit, qa, kat, va, near_bias, tq):
    bsz, s, w = qa.shape
    topk = min(TOPK_MAX, s // 4)
    lanes = V7X_LANES
    qspec = lambda n: pl.BlockSpec((1, tq, n), lambda b, i: (b, i, 0))
    resident = lambda shape: pl.BlockSpec((1,) + shape, lambda b, i: (b, 0, 0), pipeline_mode=pl.Buffered(1))
    return pl.pallas_call(
        functools.partial(_dsa_kernel, tq=tq, topk=topk),
        grid=(bsz, s // tq),
        in_specs=[qspec(w), qspec(lanes), resident((2 * IDX_DIM, s)),
                  qspec(w), resident((w, s)), resident((s, w)),
                  _const_spec(near_bias.shape)],
        out_specs=qspec(w),
        out_shape=jax.ShapeDtypeStruct((bsz, s, w), BF16),
        scratch_shapes=[pltpu.VMEM((tq, s), jnp.int32),
                        pltpu.VMEM((N_HEADS * tq, lanes), BF16),
                        pltpu.VMEM((N_HEADS, tq, lanes), BF16),
                        pltpu.VMEM((N_HEADS, tq, 1), F32),
                        pltpu.VMEM((N_HEADS, tq, 1), F32),
                        pltpu.VMEM((N_HEADS, tq, lanes), F32)],
        compiler_params=_cparams(("arbitrary", "arbitrary")),
        name="dsa",
    )(qi, wi, kit, qa, kat, va, near_bias)


DSA_T_TQ = 256
SUBLANES = 8
KEY_BITS = 32
IDX_SCALE = (IDX_DIM ** -0.5) * (N_HEADS ** -0.5)
assert DSA_T_TQ == KEY_BITS * SUBLANES, "one key block must fill the 32 bit positions of a sublane tile"


def _bit_transpose(rows):
    a = list(rows)
    j, m = KEY_BITS // 2, 0x0000FFFF
    while j:
        for k in range(KEY_BITS):
            if k & j:
                continue
            t = (a[k] ^ lax.shift_right_logical(a[k + j], jnp.int32(j))) & m
            a[k] = a[k] ^ t
            a[k + j] = a[k + j] ^ lax.shift_left(t, jnp.int32(j))
        j >>= 1
        m ^= m << j
    return a


def _dsa_t_kernel(qit_ref, wit_ref, ki_ref, qat_ref, ka_ref, vat_ref, nb_ref, o_ref,
                  sc_ref, planes_ref, alive_ref, thr_ref, cnt_ref, qim_ref, qam_ref, acc_ref, m_ref, l_ref,
                  lg_ref, bmax_ref,
                  *, tq, topk):
    i = pl.program_id(1)
    t0 = i * tq
    lanes = V7X_LANES
    sk = tq
    hd = HEAD_DIM
    idx_bits = (sc_ref.shape[0] - 1).bit_length()

    zeros = jnp.zeros((hd, tq), BF16)
    for h in range(N_HEADS):
        own = slice((h % 2) * hd, (h % 2 + 1) * hd)
        other = slice((1 - h % 2) * hd, (2 - h % 2) * hd)
        qim_ref[own, h * tq:(h + 1) * tq] = qit_ref[0, h * hd:(h + 1) * hd, :]
        qim_ref[other, h * tq:(h + 1) * tq] = zeros
        qam_ref[h, own, :] = qat_ref[0, h * hd:(h + 1) * hd, :]
        qam_ref[h, other, :] = zeros

    n_blocks = i + 1
    key_i = lax.broadcasted_iota(jnp.int32, (sk, tq), 0)
    qry_i = t0 + lax.broadcasted_iota(jnp.int32, (sk, tq), 1)
    w = wit_ref[0]

    def score_body(kb, carry):
        s0 = pl.multiple_of(kb * sk, sk)
        z = jnp.dot(ki_ref[0, pl.ds(s0, sk), :], qim_ref[...], preferred_element_type=F32)
        sc = w[0:1, :] * jnp.maximum(z[:, 0:tq], 0.0)
        for h in range(1, N_HEADS):
            sc = sc + w[h:h + 1, :] * jnp.maximum(z[:, h * tq:(h + 1) * tq], 0.0)
        sc = jnp.where(key_i + s0 <= qry_i, sc * IDX_SCALE, -jnp.inf)
        sc_ref[pl.ds(s0, sk), :] = sc
        ordered = _sortable(sc) ^ INT_MIN
        planes = _bit_transpose([ordered[g * SUBLANES:(g + 1) * SUBLANES, :] for g in range(KEY_BITS)])
        for p in range(KEY_BITS):
            planes_ref[p, kb] = planes[p]
        return carry

    lax.fori_loop(0, n_blocks, score_body, 0)

    def first_count(kb, cnt):
        alive_ref[kb] = jnp.full((SUBLANES, tq), -1, jnp.int32)
        return cnt + lax.population_count(planes_ref[0, kb])

    zero_cnt = jnp.zeros((SUBLANES, tq), jnp.int32)
    cnt0 = lax.fori_loop(0, n_blocks, first_count, zero_cnt)

    def bit_body(p, carry):
        r, k_left, cnt = carry
        total = jnp.sum(cnt, axis=0, keepdims=True)
        take = total >= k_left
        r = jnp.where(take, r | lax.shift_left(jnp.int32(1), KEY_BITS - 1 - p), r)
        k_left = jnp.where(take, k_left, k_left - total)
        drop = jnp.where(take, 0, -1)
        p_next = jnp.minimum(p + 1, KEY_BITS - 1)

        def narrow(kb, c):
            alive = alive_ref[kb] & (planes_ref[p, kb] ^ drop)
            alive_ref[kb] = alive
            return c + lax.population_count(alive & planes_ref[p_next, kb])

        return r, k_left, lax.fori_loop(0, n_blocks, narrow, zero_cnt)

    r, k_left, _ = lax.fori_loop(0, KEY_BITS, bit_body,
                                 (jnp.zeros((1, tq), jnp.int32), jnp.full((1, tq), topk, jnp.int32), cnt0))
    guess_bits = r ^ INT_MIN
    guess = pltpu.bitcast(guess_bits ^ ((guess_bits >> 31) & 0x7FFFFFFF), F32)

    def count_where(pred):
        def body(kb, c):
            hit = jnp.where(pred(sc_ref[pl.ds(pl.multiple_of(kb * sk, sk), sk), :]), 1, 0)
            for g in range(sk // SUBLANES):
                c = c + hit[g * SUBLANES:(g + 1) * SUBLANES, :]
            return c
        return jnp.sum(lax.fori_loop(0, n_blocks, body, zero_cnt), axis=0, keepdims=True)

    few = qry_i[0:1, :] + 1 < topk
    lowest = float(jnp.finfo(F32).min)
    n_ge = count_where(lambda s: s >= guess)
    n_gt = count_where(lambda s: s > guess)
    exact = jnp.logical_or(few, jnp.logical_and(n_ge >= topk, n_gt < topk))
    thr_ref[0:1, :] = jnp.where(few, lowest, guess)
    cnt_ref[0:1, :] = n_ge
    cnt_ref[1:2, :] = n_gt

    @pl.when(jnp.min(jnp.where(exact, 1, 0)) == 0)
    def _():
        def extreme(kb, c):
            blk = sc_ref[pl.ds(pl.multiple_of(kb * sk, sk), sk), :]
            return (jnp.minimum(c[0], jnp.min(jnp.where(blk > -jnp.inf, blk, jnp.inf), axis=0, keepdims=True)),
                    jnp.maximum(c[1], jnp.max(blk, axis=0, keepdims=True)))

        lo, hi = lax.fori_loop(0, n_blocks, extreme,
                               (jnp.full((1, tq), jnp.inf, F32), jnp.full((1, tq), -jnp.inf, F32)))
        top = hi

        def unsettled(state):
            lo, hi = state
            mid = 0.5 * lo + 0.5 * hi
            open_ = jnp.logical_and(jnp.logical_not(exact), jnp.logical_and(mid > lo, mid < hi))
            return jnp.max(jnp.where(open_, 1, 0)) > 0

        def halve(state):
            lo, hi = state
            mid = 0.5 * lo + 0.5 * hi
            up = count_where(lambda s: s >= mid) >= topk
            return jnp.where(up, mid, lo), jnp.where(up, hi, mid)

        lo, hi = lax.while_loop(unsettled, halve, (lo, hi))
        found = jnp.where(count_where(lambda s: s >= top) >= topk, top, lo)
        thr_new = jnp.where(exact, thr_ref[0:1, :], found)
        thr_ref[0:1, :] = thr_new
        cnt_ref[0:1, :] = count_where(lambda s: s >= thr_new)
        cnt_ref[1:2, :] = count_where(lambda s: s > thr_new)

    thr = thr_ref[0:1, :]

    keep_tied = topk - cnt_ref[1:2, :]
    surplus = jnp.logical_and(cnt_ref[0:1, :] > topk, jnp.logical_not(few))

    @pl.when(jnp.max(jnp.where(surplus, 1, 0)) > 0)
    def _():
        def tied_before(cand):
            def body(kb, c):
                s0 = pl.multiple_of(kb * sk, sk)
                hit = jnp.where(sc_ref[pl.ds(s0, sk), :] == thr, jnp.where(key_i + s0 < cand, 1, 0), 0)
                for g in range(sk // SUBLANES):
                    c = c + hit[g * SUBLANES:(g + 1) * SUBLANES, :]
                return c
            return jnp.sum(lax.fori_loop(0, n_blocks, body, zero_cnt), axis=0, keepdims=True)

        def index_bit(it, c):
            cand = c | lax.shift_left(jnp.int32(1), idx_bits - 1 - it)
            return jnp.where(tied_before(cand) < keep_tied, cand, c)

        last_kept = lax.fori_loop(0, idx_bits, index_bit, jnp.zeros((1, tq), jnp.int32))

        def remove(kb, carry):
            s0 = pl.multiple_of(kb * sk, sk)
            sc = sc_ref[pl.ds(s0, sk), :]
            drop = jnp.where(sc == thr, jnp.where(jnp.logical_and(key_i + s0 > last_kept, surplus), 1, 0), 0)
            sc_ref[pl.ds(s0, sk), :] = jnp.where(drop > 0, -jnp.inf, sc)
            return carry

        lax.fori_loop(0, n_blocks, remove, 0)

    m_ref[...] = jnp.full(m_ref.shape, NEG, F32)
    l_ref[...] = jnp.zeros(l_ref.shape, F32)
    acc_ref[...] = jnp.zeros(acc_ref.shape, F32)

    def logits(kb, slot, bias_row0):
        s0 = pl.multiple_of(kb * sk, sk)
        mask = jnp.where(sc_ref[pl.ds(s0, sk), :] >= thr, 0.0, NEG)
        for h in range(N_HEADS):
            p = h // 2
            lg = jnp.dot(ka_ref[0, pl.ds(s0, sk), p * lanes:(p + 1) * lanes], qam_ref[h],
                         preferred_element_type=F32) + mask
            if bias_row0 is not None:
                lg = lg + nb_ref[h, bias_row0:bias_row0 + sk, :]
            lg_ref[slot, h] = lg
            bmax_ref[slot, h, 0:1, :] = jnp.max(lg, axis=0, keepdims=True)

    def accumulate(kb, slot):
        s0 = pl.multiple_of(kb * sk, sk)
        for h in range(N_HEADS):
            hs = slice(h * hd, (h + 1) * hd)
            m_old = m_ref[h, 0:1, :]
            m_new = jnp.maximum(m_old, bmax_ref[slot, h, 0:1, :])
            alpha = jnp.exp(m_old - m_new)
            pr = jnp.exp(lg_ref[slot, h] - m_new)
            l_ref[h, 0:1, :] = alpha * l_ref[h, 0:1, :] + jnp.sum(pr, axis=0, keepdims=True)
            acc_ref[hs, :] = alpha * acc_ref[hs, :] + jnp.dot(
                vat_ref[0, hs, pl.ds(s0, sk)], pr.astype(BF16), preferred_element_type=F32)
            m_ref[h, 0:1, :] = m_new

    n_far = jnp.maximum(i - 1, 0)

    @pl.when(n_far > 0)
    def _():
        logits(0, 0, None)

    def far_body(kb, carry):
        logits(kb, kb % 2, None)
        accumulate(kb - 1, (kb - 1) % 2)
        return carry

    lax.fori_loop(1, n_far, far_body, 0)

    @pl.when(n_far > 0)
    def _():
        accumulate(n_far - 1, (n_far - 1) % 2)

    @pl.when(i > 0)
    def _():
        logits(i - 1, 0, 0)
        accumulate(i - 1, 0)

    logits(i, 1, sk)
    accumulate(i, 1)

    for h in range(N_HEADS):
        hs = slice(h * hd, (h + 1) * hd)
        acc_ref[hs, :] = acc_ref[hs, :] / l_ref[h, 0:1, :]
    o_ref[0] = acc_ref[...].T.astype(o_ref.dtype)


def _dsa_t(qit, wit, ki, qat, ka, vat, near_bias, tq):
    bsz, w, s = qat.shape
    topk = min(TOPK_MAX, s // 4)
    tspec = lambda n: pl.BlockSpec((1, n, tq), lambda b, i: (b, 0, i))
    resident = lambda shape: pl.BlockSpec((1,) + shape, lambda b, i: (b, 0, 0), pipeline_mode=pl.Buffered(1))
    return pl.pallas_call(
        functools.partial(_dsa_t_kernel, tq=tq, topk=topk),
        grid=(bsz, s // tq),
        in_specs=[tspec(w), tspec(WI_ROWS), resident((s, 2 * IDX_DIM)),
                  tspec(w), resident((s, w)), resident((w, s)),
                  _const_spec(near_bias.shape)],
        out_specs=pl.BlockSpec((1, tq, w), lambda b, i: (b, i, 0)),
        out_shape=jax.ShapeDtypeStruct((bsz, s, w), BF16),
        scratch_shapes=[pltpu.VMEM((s, tq), F32),
                        pltpu.VMEM((KEY_BITS, s // tq, SUBLANES, tq), jnp.int32),
                        pltpu.VMEM((s // tq, SUBLANES, tq), jnp.int32),
                        pltpu.VMEM((SUBLANES, tq), F32),
                        pltpu.VMEM((SUBLANES, tq), jnp.int32),
                        pltpu.VMEM((2 * HEAD_DIM, N_HEADS * tq), BF16),
                        pltpu.VMEM((N_HEADS, 2 * HEAD_DIM, tq), BF16),
                        pltpu.VMEM((w, tq), F32),
                        pltpu.VMEM((N_HEADS, SUBLANES, tq), F32),
                        pltpu.VMEM((N_HEADS, SUBLANES, tq), F32),
                        pltpu.VMEM((2, N_HEADS, tq, tq), F32),
                        pltpu.VMEM((2, N_HEADS, SUBLANES, tq), F32)],
        compiler_params=_cparams(("arbitrary", "arbitrary")),
        name="dsa",
    )(qit, wit, ki, qat, ka, vat, near_bias)


SB_TQ = 256
SB_DEAD = -104.0


def _sb_kernel(q_ref, kt_ref, v_ref, o_ref, *, tq):
    i = pl.program_id(2)
    lanes = V7X_LANES
    lo_half = lax.broadcasted_iota(jnp.int32, (tq, lanes), 1) < HEAD_DIM
    r_i = lax.broadcasted_iota(jnp.int32, (tq, tq), 0)
    c_i = lax.broadcasted_iota(jnp.int32, (tq, tq), 1)
    strict = c_i < r_i
    after = jnp.where(r_i > c_i, 1.0, 0.0).astype(BF16)
    after2 = jnp.concatenate([after, after], axis=0)
    q = q_ref[0].astype(F32)
    qms = [jnp.where(lo_half, q, 0.0).astype(BF16), jnp.where(lo_half, 0.0, q).astype(BF16)]

    def tile(kb, diag, state):
        s0 = pl.multiple_of(kb * tq, tq)
        kt = kt_ref[0, :, pl.ds(s0, tq)]
        v = v_ref[0, pl.ds(s0, tq), :]
        carries, accs = [], []
        for hh in range(2):
            carry, acc = state[hh], state[2 + hh]
            z = jnp.dot(qms[hh], kt, preferred_element_type=F32)
            sp = jnp.log(1.0 + jnp.exp(-jnp.abs(z)))
            log_beta = jnp.minimum(z, 0.0) - sp
            log_rest = -(jnp.maximum(z, 0.0) + sp)
            if diag:
                log_rest = jnp.where(strict, log_rest, 0.0)
            hi = log_rest.astype(BF16)
            lo = (log_rest - hi.astype(F32)).astype(BF16)
            suffix = jnp.dot(jnp.concatenate([hi, lo], axis=1), after2, preferred_element_type=F32)
            a = jnp.exp(log_beta + suffix + carry)
            if diag:
                a = jnp.where(strict, a, 0.0)
            accs.append(acc + jnp.dot(a.astype(BF16), v, preferred_element_type=F32))
            carries.append(carry + jnp.sum(log_rest, axis=1, keepdims=True))
        return tuple(carries) + tuple(accs)

    zc = jnp.zeros((tq, 1), F32)
    za = jnp.zeros((tq, lanes), F32)
    state = tile(i, True, (zc, zc, za, za))

    def alive(ls):
        kb, state = ls
        worst = jnp.max(jnp.maximum(state[0], state[1]))
        return jnp.logical_and(kb >= 0, worst >= SB_DEAD)

    def body(ls):
        kb, state = ls
        return kb - 1, tile(kb, False, state)

    _, state = lax.while_loop(alive, body, (i - 1, state))
    o_ref[0] = jnp.where(lo_half, state[2], state[3]).astype(o_ref.dtype)


def _stick_breaking(qb, kbt, vb, tq=SB_TQ):
    bsz, s, w = qb.shape
    lanes = V7X_LANES
    return pl.pallas_call(
        functools.partial(_sb_kernel, tq=tq),
        grid=(bsz, w // lanes, s // tq),
        in_specs=[pl.BlockSpec((1, tq, lanes), lambda b, p, i: (b, i, p)),
                  pl.BlockSpec((1, lanes, s), lambda b, p, i: (b, p, 0)),
                  pl.BlockSpec((1, s, lanes), lambda b, p, i: (b, 0, p))],
        out_specs=pl.BlockSpec((1, tq, lanes), lambda b, p, i: (b, i, p)),
        out_shape=jax.ShapeDtypeStruct((bsz, s, w), BF16),
        compiler_params=_cparams(("arbitrary", "arbitrary", "arbitrary")),
        name="stick_breaking",
    )(qb, kbt, vb)


PEER_HEADS = 8
N_KEYS = 128
PEER_HALF = 128
PEER_TOPK = 16
PEER_GROUPS = 2 * PEER_HEADS


def _mix_kernel(x_ref, ya_ref, yb_ref, ga_ref, gb_ref, mod_ref, g2_ref, wa_ref, wb_ref, wo_ref, wq_ref,
                sk_ref, x1_ref, hn2t_ref, st_ref):
    ma = jnp.dot(ya_ref[0], wa_ref[...], preferred_element_type=F32)
    mb = jnp.dot(yb_ref[0], wb_ref[...], preferred_element_type=F32)
    merged = jax.nn.sigmoid(ga_ref[0]) * ma + jax.nn.sigmoid(gb_ref[0]) * mb
    x1 = x_ref[0] + mod_ref[0, 2:3, :] * jnp.dot(merged.astype(BF16), wo_ref[...], preferred_element_type=F32)
    x1_ref[0] = x1
    h2 = _modulated_norm(x1, g2_ref[...], mod_ref[0, 3:4, :], mod_ref[0, 4:5, :])
    hn2t_ref[0] = h2.T.astype(BF16)
    qh = jnp.dot(h2.astype(BF16), wq_ref[...], preferred_element_type=F32).astype(BF16)
    nt = (((1,), (1,)), ((), ()))
    for g in range(PEER_GROUPS):
        st_ref[0, g * N_KEYS:(g + 1) * N_KEYS, :] = lax.dot_general(
            sk_ref[g], qh[:, g * PEER_HALF:(g + 1) * PEER_HALF], nt, preferred_element_type=F32)


def _mix(x, ya, yb, ga, gb, mod, g2, w_proj_a, w_proj_b, w_out, peer_wq, sub_keys, tm=512):
    bsz, s, d = x.shape
    w = ya.shape[-1]
    nq = peer_wq.shape[1]
    sk = sub_keys.reshape(PEER_GROUPS, N_KEYS, PEER_HALF).astype(BF16)
    rspec = lambda n: pl.BlockSpec((1, tm, n), lambda b, i: (b, i, 0))
    tspec = lambda n: pl.BlockSpec((1, n, tm), lambda b, i: (b, 0, i))
    return pl.pallas_call(
        _mix_kernel,
        grid=(bsz, s // tm),
        in_specs=[rspec(d), rspec(w), rspec(w), rspec(d), rspec(d),
                  pl.BlockSpec((1, 6, d), lambda b, i: (b, 0, 0)), _const_spec((1, d)),
                  _const_spec((w, d)), _const_spec((w, d)), _const_spec((d, d)), _const_spec((d, nq)),
                  _const_spec(sk.shape)],
        out_specs=[rspec(d), tspec(d), tspec(PEER_GROUPS * N_KEYS)],
        out_shape=[jax.ShapeDtypeStruct((bsz, s, d), F32),
                   jax.ShapeDtypeStruct((bsz, d, s), BF16),
                   jax.ShapeDtypeStruct((bsz, PEER_GROUPS * N_KEYS, s), F32)],
        compiler_params=_cparams(("parallel", "parallel")),
        name="mix_out",
    )(x, ya, yb, ga, gb, mod, g2.reshape(1, d), w_proj_a.astype(BF16), w_proj_b.astype(BF16),
      w_out.astype(BF16), peer_wq.astype(BF16), sk)


N_STATS = 8
SELECT_WAYS = 4
PEER_CAND_ROWS = -(-sum(PEER_TOPK // (k + 1) for k in range(PEER_TOPK)) // 8) * 8


def _peer_select_kernel(st_ref, stats_ref, tops_ref, cand_ref):
    tn = st_ref.shape[-1]
    lanes = V7X_LANES
    top_ref = tops_ref.at[0]

    def chunk_body(c, carry):
        cs = pl.ds(pl.multiple_of(c * lanes, lanes), lanes)

        def groups_body(gq, carry):
            gs = [gq * SELECT_WAYS + w for w in range(SELECT_WAYS)]
            sc = [st_ref[0, pl.ds(pl.multiple_of(g * N_KEYS, N_KEYS), N_KEYS), cs] for g in gs]
            for k in range(PEER_TOPK):
                for w, g in enumerate(gs):
                    m = jnp.max(sc[w], axis=0, keepdims=True)
                    top_ref[g, k:k + 1, cs] = m
                    sc[w] = jnp.where(sc[w] == m, -jnp.inf, sc[w])
            return carry

        lax.fori_loop(0, PEER_GROUPS // SELECT_WAYS, groups_body, 0)

        def heads_body(hq, carry):
            hs = [hq * 2, hq * 2 + 1]
            v1 = [top_ref[2 * h, :, cs] for h in hs]
            v2 = [top_ref[2 * h + 1, :, cs] for h in hs]
            cand = []
            for w in range(2):
                cand_ref[w] = jnp.full(cand_ref.shape[1:], -jnp.inf, F32)
                row = 0
                for k in range(PEER_TOPK):
                    n = PEER_TOPK // (k + 1)
                    cand_ref[w, row:row + n, :] = v1[w][k:k + 1, :] + v2[w][0:n, :]
                    row += n
                cand.append(cand_ref[w])
            sc = list(cand)
            for k in range(PEER_TOPK):
                mk = [jnp.max(sc[w], axis=0, keepdims=True) for w in range(2)]
                sc = [jnp.where(sc[w] == mk[w], -jnp.inf, sc[w]) for w in range(2)]
            for w, h in enumerate(hs):
                thr = mk[w]
                m = v1[w][0:1, :] + v2[w][0:1, :]
                z = jnp.sum(jnp.where(cand[w] >= thr, jnp.exp(cand[w] - m), 0.0), axis=0, keepdims=True)
                rows = [v1[w][PEER_TOPK - 1:PEER_TOPK, :], v2[w][PEER_TOPK - 1:PEER_TOPK, :], thr,
                        v1[w][0:1, :], v2[w][0:1, :], 1.0 / z, jnp.zeros((2, lanes), F32)]
                stats_ref[0, pl.ds(pl.multiple_of(h * N_STATS, N_STATS), N_STATS), cs] = jnp.concatenate(rows, axis=0)
            return carry

        lax.fori_loop(0, PEER_HEADS // 2, heads_body, 0)
        return carry

    lax.fori_loop(0, tn // lanes, chunk_body, 0)


def _peer_select(st, tn=512):
    bsz, rows, s = st.shape
    return pl.pallas_call(
        _peer_select_kernel,
        grid=(bsz, s // tn),
        in_specs=[pl.BlockSpec((1, rows, tn), lambda b, i: (b, 0, i))],
        out_specs=[pl.BlockSpec((1, PEER_HEADS * N_STATS, tn), lambda b, i: (b, 0, i)),
                   pl.BlockSpec((1, PEER_GROUPS, PEER_TOPK, tn), lambda b, i: (b, 0, 0, i))],
        out_shape=[jax.ShapeDtypeStruct((bsz, PEER_HEADS * N_STATS, s), F32),
                   jax.ShapeDtypeStruct((bsz, PEER_GROUPS, PEER_TOPK, s), F32)],
        scratch_shapes=[pltpu.VMEM((2, PEER_CAND_ROWS, V7X_LANES), F32)],
        compiler_params=_cparams(("parallel", "parallel")),
        name="peer_select",
    )(st)


def _peer_experts_kernel(ht_ref, st_ref, stats_ref, tops_ref, u_ref, vt_ref, o_ref,
                         len1_ref, e1_ref, rank2_ref, e2_ref, pre_ref, coef_ref, *, te, n_chunks):
    j = pl.program_id(2)
    tm = ht_ref.shape[-1]
    lanes = V7X_LANES
    rows_per_step = te // N_KEYS

    @pl.when(j == 0)
    def _():
        o_ref[...] = jnp.zeros(o_ref.shape, F32)
        coef_ref[1] = jnp.zeros(coef_ref.shape[1:], BF16)
        for h in range(PEER_HEADS):
            st = lambda k: stats_ref[0, h * N_STATS + k:h * N_STATS + k + 1, :]
            s1 = st_ref[0, (2 * h) * N_KEYS:(2 * h + 1) * N_KEYS, :]
            s2 = st_ref[0, (2 * h + 1) * N_KEYS:(2 * h + 2) * N_KEYS, :]
            v2 = tops_ref[0, 2 * h + 1]
            in1 = s1 >= st(0)
            in2 = s2 >= st(1)
            a1 = jnp.where(in1, s1, NEG)
            len1 = jnp.zeros(s1.shape, F32)
            rank2 = jnp.zeros(s2.shape, F32)
            for l in range(PEER_TOPK):
                len1 = len1 + jnp.where(a1 + v2[l:l + 1, :] >= st(2), 1.0, 0.0)
                rank2 = rank2 + jnp.where(v2[l:l + 1, :] > s2, 1.0, 0.0)
            len1_ref[h] = len1
            e1_ref[h] = jnp.where(in1, jnp.exp(s1 - st(3)) * st(5), 0.0)
            rank2_ref[h] = jnp.where(in2, rank2, float(PEER_TOPK)).astype(BF16)
            e2_ref[h] = jnp.where(in2, jnp.exp(s2 - st(4)), 0.0).astype(BF16)

    def add_previous_chunk():
        o_ref[0] += jnp.dot(vt_ref[...], coef_ref[(j + 1) % 2], preferred_element_type=F32)

    @pl.when(j < n_chunks)
    def _():
        ht = ht_ref[0]
        for r in range(rows_per_step):
            rs = slice(r * N_KEYS, (r + 1) * N_KEYS)
            pre_ref[rs, :] = jnp.dot(u_ref[rs, :], ht, preferred_element_type=F32)
        add_previous_chunk()
        first = pl.multiple_of(j * rows_per_step, rows_per_step)
        slot = j % 2
        zero = jnp.zeros((N_KEYS, lanes), BF16)
        for c in range(tm // lanes):
            cs = slice(c * lanes, (c + 1) * lanes)
            for r in range(rows_per_step):
                rs = slice(r * N_KEYS, (r + 1) * N_KEYS)
                gate = zero
                for h in range(PEER_HEADS):
                    row = lambda ref: jnp.broadcast_to(ref[h, pl.ds(first, rows_per_step), cs][r:r + 1, :],
                                                       (N_KEYS, lanes)).astype(BF16)
                    gate = gate + jnp.where(rank2_ref[h, :, cs] < row(len1_ref), row(e1_ref) * e2_ref[h, :, cs], zero)
                x = pre_ref[rs, cs]
                gelu = 0.5 * x * (1.0 + lax.erf(x * np.float32(math.sqrt(0.5))))
                coef_ref[slot, rs, cs] = gate * gelu.astype(BF16)

    @pl.when(j == n_chunks)
    def _():
        add_previous_chunk()


def _peer_experts(hn2t, st, stats, tops, peer_u, peer_v, tm=512, te=1024):
    bsz, d, s = hn2t.shape
    n_exp = peer_u.shape[0]
    n_chunks = n_exp // te
    u = peer_u.astype(BF16)
    vt = peer_v.T.astype(BF16)
    tok = lambda n: pl.BlockSpec((1, n, tm), lambda b, i, j: (b, 0, i))
    table = lambda dt: pltpu.VMEM((PEER_HEADS, N_KEYS, tm), dt)
    return pl.pallas_call(
        functools.partial(_peer_experts_kernel, te=te, n_chunks=n_chunks),
        grid=(bsz, s // tm, n_chunks + 1),
        in_specs=[tok(d), tok(st.shape[1]), tok(stats.shape[1]),
                  pl.BlockSpec((1, PEER_GROUPS, PEER_TOPK, tm), lambda b, i, j: (b, 0, 0, i)),
                  pl.BlockSpec((te, d), lambda b, i, j: (jnp.minimum(j, n_chunks - 1), 0)),
                  pl.BlockSpec((d, te), lambda b, i, j: (0, jnp.maximum(j - 1, 0)))],
        out_specs=tok(d),
        out_shape=jax.ShapeDtypeStruct((bsz, d, s), F32),
        scratch_shapes=[table(F32), table(F32), table(BF16), table(BF16),
                        pltpu.VMEM((te, tm), F32), pltpu.VMEM((2, te, tm), BF16)],
        compiler_params=_cparams(("parallel", "parallel", "arbitrary")),
        name="peer_experts",
    )(hn2t, st, stats, tops, u, vt)


def _final_kernel(x1_ref, ft_ref, mod_ref, g_ref, o_ref):
    x2 = x1_ref[0] + mod_ref[0, 5:6, :] * ft_ref[0].T
    y = x2 * lax.rsqrt(jnp.mean(x2 * x2, axis=-1, keepdims=True) + EPS)
    o_ref[0] = y * g_ref[...]


def _final(x1, ft, mod, final_g, tm=512):
    bsz, s, d = x1.shape
    return pl.pallas_call(
        _final_kernel,
        grid=(bsz, s // tm),
        in_specs=[pl.BlockSpec((1, tm, d), lambda b, i: (b, i, 0)),
                  pl.BlockSpec((1, d, tm), lambda b, i: (b, 0, i)),
                  pl.BlockSpec((1, 6, d), lambda b, i: (b, 0, 0)), _const_spec((1, d))],
        out_specs=pl.BlockSpec((1, tm, d), lambda b, i: (b, i, 0)),
        out_shape=jax.ShapeDtypeStruct((bsz, s, d), F32),
        compiler_params=_cparams(("parallel", "parallel")),
        name="final_norm",
    )(x1, ft, mod, final_g.reshape(1, d))


def kernel(x, c, w_ada, b_ada, norm1_g, norm2_g, w_in, rel_bias, w_proj_a, w_proj_b, w_out,
           peer_wq, peer_sub_keys, peer_u, peer_v, final_g):
    bsz, s, d = x.shape
    mod = _ada(c, w_ada[0], b_ada[0]).reshape(bsz, 6, d)
    ka, ki, qb, vb, ga, gb, qat, vat, qit, wit, kbt = _in_proj(x, mod, norm1_g[0], w_in[0])
    ya = _dsa_t(qit, wit, ki, qat, ka, vat, _near_bias(rel_bias, DSA_T_TQ), DSA_T_TQ)
    yb = _stick_breaking(qb, kbt, vb)
    x1, hn2t, st = _mix(x, ya, yb, ga, gb, mod, norm2_g[0], w_proj_a[0], w_proj_b[0], w_out[0],
                        peer_wq[0], peer_sub_keys[0])
    stats, tops = _peer_select(st)
    ft = _peer_experts(hn2t, st, stats, tops, peer_u[0], peer_v[0])
    return _final(x1, ft, mod, final_g)
```

```python
import functools
import math

import jax
import jax.numpy as jnp
import numpy as np
from jax import lax
from jax.experimental import pallas as pl
from jax.experimental.pallas import tpu as pltpu

N_HEADS = 8
HEAD_DIM = 64
WIDTH = N_HEADS * HEAD_DIM
IDX_DIM = 64
TOPK_MAX = 256
N_BUCKETS = 32
MAX_DISTANCE = 128
EPS = 1e-6

V7X_LANES = 128
V7X_VMEM_LIMIT = 56 * 1024 * 1024

F32 = jnp.float32
BF16 = jnp.bfloat16


def _cparams(sem):
    return pltpu.CompilerParams(dimension_semantics=sem, vmem_limit_bytes=V7X_VMEM_LIMIT)


def _const_spec(shape):
    return pl.BlockSpec(shape, lambda *_: (0,) * len(shape), pipeline_mode=pl.Buffered(1))


def _ada_kernel(c_ref, w_ref, b_ref, o_ref):
    c = c_ref[...]
    cond = c * jax.nn.sigmoid(c)
    o_ref[...] = jnp.dot(cond, w_ref[...], preferred_element_type=F32,
                         precision=lax.Precision.HIGHEST) + b_ref[...]


def _ada(c, w, b):
    bsz, d = c.shape
    n = w.shape[1]
    tn = 1536
    return pl.pallas_call(
        _ada_kernel,
        grid=(n // tn,),
        in_specs=[pl.BlockSpec((bsz, d), lambda j: (0, 0)),
                  pl.BlockSpec((d, tn), lambda j: (0, j)),
                  pl.BlockSpec((1, tn), lambda j: (0, j))],
        out_specs=pl.BlockSpec((bsz, tn), lambda j: (0, j)),
        out_shape=jax.ShapeDtypeStruct((bsz, n), F32),
        compiler_params=_cparams(("arbitrary",)),
        name="ada",
    )(c, w, b.reshape(1, n))


def _modulated_norm(x, g, shift, scale):
    y = x * lax.rsqrt(jnp.mean(x * x, axis=-1, keepdims=True) + EPS)
    return (y * g) * (1.0 + scale) + shift


WI_ROWS = 16


def _in_proj_kernel(x_ref, mod_ref, g_ref, wr_ref, wt_ref,
                    ka_ref, ki_ref, qb_ref, vb_ref, ga_ref, gb_ref,
                    qat_ref, vat_ref, qit_ref, wit_ref, kbt_ref):
    h = _modulated_norm(x_ref[0], g_ref[...], mod_ref[0, 0:1, :], mod_ref[0, 1:2, :]).astype(BF16)
    c0 = 0
    for ref in (ka_ref, ki_ref, qb_ref, vb_ref, ga_ref, gb_ref):
        n = ref.shape[-1]
        for part in range(0, n, WIDTH):
            pw = min(WIDTH, n - part)
            ref[0, :, part:part + pw] = jnp.dot(h, wr_ref[:, c0 + part:c0 + part + pw],
                                               preferred_element_type=F32).astype(ref.dtype)
        c0 += n
    nt = (((1,), (1,)), ((), ()))
    r0 = 0
    for ref in (qat_ref, vat_ref, qit_ref, wit_ref, kbt_ref):
        n = ref.shape[1]
        ref[0] = lax.dot_general(wt_ref[r0:r0 + n, :], h, nt, preferred_element_type=F32).astype(ref.dtype)
        r0 += n


def _in_proj(x, mod, g1, w_in, tm=512):
    bsz, s, d = x.shape
    w = WIDTH
    offs = np.cumsum([0, w, w, w, w, IDX_DIM, N_HEADS, w, w, w, d, d])
    col = lambda i: w_in[:, offs[i]:offs[i + 1]]
    qscale = HEAD_DIM ** -0.5
    w_rows = jnp.concatenate([col(1), col(4), col(4), col(6) * qscale, col(8), col(9), col(10)],
                             axis=1).astype(BF16)
    w_t = jnp.concatenate([col(0) * qscale, col(2), col(3),
                           jnp.pad(col(5), ((0, 0), (0, WI_ROWS - N_HEADS))), col(7)], axis=1).T.astype(BF16)

    row = lambda n, dt: jax.ShapeDtypeStruct((bsz, s, n), dt)
    tr = lambda n, dt: jax.ShapeDtypeStruct((bsz, n, s), dt)
    rspec = lambda n: pl.BlockSpec((1, tm, n), lambda b, i: (b, i, 0))
    tspec = lambda n: pl.BlockSpec((1, n, tm), lambda b, i: (b, 0, i))
    return pl.pallas_call(
        _in_proj_kernel,
        grid=(bsz, s // tm),
        in_specs=[rspec(d),
                  pl.BlockSpec((1, 6, d), lambda b, i: (b, 0, 0)),
                  _const_spec((1, d)),
                  _const_spec(w_rows.shape), _const_spec(w_t.shape)],
        out_specs=[rspec(w), rspec(2 * IDX_DIM), rspec(w), rspec(w), rspec(d), rspec(d),
                   tspec(w), tspec(w), tspec(w), tspec(WI_ROWS), tspec(w)],
        out_shape=[row(w, BF16), row(2 * IDX_DIM, BF16), row(w, BF16), row(w, BF16), row(d, F32), row(d, F32),
                   tr(w, BF16), tr(w, BF16), tr(w, BF16), tr(WI_ROWS, F32), tr(w, BF16)],
        compiler_params=_cparams(("parallel", "parallel")),
        name="in_proj",
    )(x, mod, g1.reshape(1, d), w_rows, w_t)


def _t5_bucket_np(n):
    max_exact = N_BUCKETS // 2
    nf = np.maximum(n, 1).astype(np.float32)
    large = max_exact + (np.log(nf / np.float32(max_exact)) / np.float32(math.log(MAX_DISTANCE / max_exact))
                         * np.float32(N_BUCKETS - max_exact)).astype(np.int32)
    large = np.minimum(large, N_BUCKETS - 1)
    return np.where(n < max_exact, n, large).astype(np.int32)


def _near_bias_kernel(bk_ref, rb_ref, o_ref):
    bk = bk_ref[...]
    for h in range(N_HEADS):
        far = rb_ref[N_BUCKETS - 1, h]
        acc = jnp.zeros(bk.shape, F32)
        for b in range(N_BUCKETS - 1):
            acc = jnp.where(bk == b, rb_ref[b, h] - far, acc)
        o_ref[h] = acc


def _near_bias(rel_bias, tq):
    a = np.arange(2 * tq)[:, None]
    b = np.arange(tq)[None, :]
    n = tq + b - a
    far_from = int(np.argmax(_t5_bucket_np(np.arange(4 * MAX_DISTANCE)) == N_BUCKETS - 1))
    assert far_from <= tq, "distances beyond the near zone must share the last bucket"
    bucket = np.where(n >= 0, _t5_bucket_np(np.maximum(n, 0)), N_BUCKETS - 1).astype(np.int32)
    return pl.pallas_call(
        _near_bias_kernel,
        in_specs=[pl.BlockSpec(memory_space=pltpu.VMEM), pl.BlockSpec(memory_space=pltpu.SMEM)],
        out_specs=pl.BlockSpec(memory_space=pltpu.VMEM),
        out_shape=jax.ShapeDtypeStruct((N_HEADS, 2 * tq, tq), F32),
        name="near_bias",
    )(jnp.asarray(bucket), rel_bias)


INT_MIN = -2 ** 31
NEG = -1e30


def _sortable(x):
    bits = pltpu.bitcast(x, jnp.int32)
    return bits ^ ((bits >> 31) & 0x7FFFFFFF)


DSA_T_TQ = 256
SUBLANES = 8
KEY_BITS = 32
IDX_SCALE = (IDX_DIM ** -0.5) * (N_HEADS ** -0.5)
assert DSA_T_TQ == KEY_BITS * SUBLANES, "one key block must fill the 32 bit positions of a sublane tile"


def _bit_transpose(rows):
    a = list(rows)
    j, m = KEY_BITS // 2, 0x0000FFFF
    while j:
        for k in range(KEY_BITS):
            if k & j:
                continue
            t = (a[k] ^ lax.shift_right_logical(a[k + j], jnp.int32(j))) & m
            a[k] = a[k] ^ t
            a[k + j] = a[k + j] ^ lax.shift_left(t, jnp.int32(j))
        j >>= 1
        m ^= m << j
    return a


def _dsa_t_kernel(qit_ref, wit_ref, ki_ref, qat_ref, ka_ref, vat_ref, nb_ref, o_ref,
                  sc_ref, planes_ref, alive_ref, thr_ref, cnt_ref, qim_ref, qam_ref, acc_ref, m_ref, l_ref,
                  lg_ref, bmax_ref,
                  *, tq, topk):
    i = pl.program_id(1)
    t0 = i * tq
    lanes = V7X_LANES
    sk = tq
    hd = HEAD_DIM
    idx_bits = (sc_ref.shape[0] - 1).bit_length()

    zeros = jnp.zeros((hd, tq), BF16)
    for h in range(N_HEADS):
        own = slice((h % 2) * hd, (h % 2 + 1) * hd)
        other = slice((1 - h % 2) * hd, (2 - h % 2) * hd)
        qim_ref[own, h * tq:(h + 1) * tq] = qit_ref[0, h * hd:(h + 1) * hd, :]
        qim_ref[other, h * tq:(h + 1) * tq] = zeros
        qam_ref[h, own, :] = qat_ref[0, h * hd:(h + 1) * hd, :]
        qam_ref[h, other, :] = zeros

    n_blocks = i + 1
    key_i = lax.broadcasted_iota(jnp.int32, (sk, tq), 0)
    qry_i = t0 + lax.broadcasted_iota(jnp.int32, (sk, tq), 1)
    w = wit_ref[0]

    def score_body(kb, carry):
        s0 = pl.multiple_of(kb * sk, sk)
        z = jnp.dot(ki_ref[0, pl.ds(s0, sk), :], qim_ref[...], preferred_element_type=F32)
        sc = w[0:1, :] * jnp.maximum(z[:, 0:tq], 0.0)
        for h in range(1, N_HEADS):
            sc = sc + w[h:h + 1, :] * jnp.maximum(z[:, h * tq:(h + 1) * tq], 0.0)
        sc = jnp.where(key_i + s0 <= qry_i, sc * IDX_SCALE, -jnp.inf)
        sc_ref[pl.ds(s0, sk), :] = sc
        ordered = _sortable(sc) ^ INT_MIN
        planes = _bit_transpose([ordered[g * SUBLANES:(g + 1) * SUBLANES, :] for g in range(KEY_BITS)])
        for p in range(KEY_BITS):
            planes_ref[p, kb] = planes[p]
        return carry

    lax.fori_loop(0, n_blocks, score_body, 0)

    def first_count(kb, cnt):
        alive_ref[kb] = jnp.full((SUBLANES, tq), -1, jnp.int32)
        return cnt + lax.population_count(planes_ref[0, kb])

    zero_cnt = jnp.zeros((SUBLANES, tq), jnp.int32)
    cnt0 = lax.fori_loop(0, n_blocks, first_count, zero_cnt)

    def bit_body(p, carry):
        r, k_left, cnt = carry
        total = jnp.sum(cnt, axis=0, keepdims=True)
        take = total >= k_left
        r = jnp.where(take, r | lax.shift_left(jnp.int32(1), KEY_BITS - 1 - p), r)
        k_left = jnp.where(take, k_left, k_left - total)
        drop = jnp.where(take, 0, -1)
        p_next = jnp.minimum(p + 1, KEY_BITS - 1)

        def narrow(kb, c):
            alive = alive_ref[kb] & (planes_ref[p, kb] ^ drop)
            alive_ref[kb] = alive
            return c + lax.population_count(alive & planes_ref[p_next, kb])

        return r, k_left, lax.fori_loop(0, n_blocks, narrow, zero_cnt)

    r, k_left, _ = lax.fori_loop(0, KEY_BITS, bit_body,
                                 (jnp.zeros((1, tq), jnp.int32), jnp.full((1, tq), topk, jnp.int32), cnt0))
    guess_bits = r ^ INT_MIN
    guess = pltpu.bitcast(guess_bits ^ ((guess_bits >> 31) & 0x7FFFFFFF), F32)

    def count_where(pred):
        def body(kb, c):
            hit = jnp.where(pred(sc_ref[pl.ds(pl.multiple_of(kb * sk, sk), sk), :]), 1, 0)
            for g in range(sk // SUBLANES):
                c = c + hit[g * SUBLANES:(g + 1) * SUBLANES, :]
            return c
        return jnp.sum(lax.fori_loop(0, n_blocks, body, zero_cnt), axis=0, keepdims=True)

    few = qry_i[0:1, :] + 1 < topk
    lowest = float(jnp.finfo(F32).min)
    n_ge = count_where(lambda s: s >= guess)
    n_gt = count_where(lambda s: s > guess)
    exact = jnp.logical_or(few, jnp.logical_and(n_ge >= topk, n_gt < topk))
    thr_ref[0:1, :] = jnp.where(few, lowest, guess)
    cnt_ref[0:1, :] = n_ge
    cnt_ref[1:2, :] = n_gt

    @pl.when(jnp.min(jnp.where(exact, 1, 0)) == 0)
    def _():
        def extreme(kb, c):
            blk = sc_ref[pl.ds(pl.multiple_of(kb * sk, sk), sk), :]
            return (jnp.minimum(c[0], jnp.min(jnp.where(blk > -jnp.inf, blk, jnp.inf), axis=0, keepdims=True)),
                    jnp.maximum(c[1], jnp.max(blk, axis=0, keepdims=True)))

        lo, hi = lax.fori_loop(0, n_blocks, extreme,
                               (jnp.full((1, tq), jnp.inf, F32), jnp.full((1, tq), -jnp.inf, F32)))
        top = hi

        def unsettled(state):
            lo, hi = state
            mid = 0.5 * lo + 0.5 * hi
            open_ = jnp.logical_and(jnp.logical_not(exact), jnp.logical_and(mid > lo, mid < hi))
            return jnp.max(jnp.where(open_, 1, 0)) > 0

        def halve(state):
            lo, hi = state
            mid = 0.5 * lo + 0.5 * hi
            up = count_where(lambda s: s >= mid) >= topk
            return jnp.where(up, mid, lo), jnp.where(up, hi, mid)

        lo, hi = lax.while_loop(unsettled, halve, (lo, hi))
        found = jnp.where(count_where(lambda s: s >= top) >= topk, top, lo)
        thr_new = jnp.where(exact, thr_ref[0:1, :], found)
        thr_ref[0:1, :] = thr_new
        cnt_ref[0:1, :] = count_where(lambda s: s >= thr_new)
        cnt_ref[1:2, :] = count_where(lambda s: s > thr_new)

    thr = thr_ref[0:1, :]

    keep_tied = topk - cnt_ref[1:2, :]
    surplus = jnp.logical_and(cnt_ref[0:1, :] > topk, jnp.logical_not(few))

    @pl.when(jnp.max(jnp.where(surplus, 1, 0)) > 0)
    def _():
        def tied_before(cand):
            def body(kb, c):
                s0 = pl.multiple_of(kb * sk, sk)
                hit = jnp.where(sc_ref[pl.ds(s0, sk), :] == thr, jnp.where(key_i + s0 < cand, 1, 0), 0)
                for g in range(sk // SUBLANES):
                    c = c + hit[g * SUBLANES:(g + 1) * SUBLANES, :]
                return c
            return jnp.sum(lax.fori_loop(0, n_blocks, body, zero_cnt), axis=0, keepdims=True)

        def index_bit(it, c):
            cand = c | lax.shift_left(jnp.int32(1), idx_bits - 1 - it)
            return jnp.where(tied_before(cand) < keep_tied, cand, c)

        last_kept = lax.fori_loop(0, idx_bits, index_bit, jnp.zeros((1, tq), jnp.int32))

        def remove(kb, carry):
            s0 = pl.multiple_of(kb * sk, sk)
            sc = sc_ref[pl.ds(s0, sk), :]
            drop = jnp.where(sc == thr, jnp.where(jnp.logical_and(key_i + s0 > last_kept, surplus), 1, 0), 0)
            sc_ref[pl.ds(s0, sk), :] = jnp.where(drop > 0, -jnp.inf, sc)
            return carry

        lax.fori_loop(0, n_blocks, remove, 0)

    m_ref[...] = jnp.full(m_ref.shape, NEG, F32)
    l_ref[...] = jnp.zeros(l_ref.shape, F32)
    acc_ref[...] = jnp.zeros(acc_ref.shape, F32)

    def logits(kb, slot, bias_row0):
        s0 = pl.multiple_of(kb * sk, sk)
        mask = jnp.where(sc_ref[pl.ds(s0, sk), :] >= thr, 0.0, NEG)
        for h in range(N_HEADS):
            p = h // 2
            lg = jnp.dot(ka_ref[0, pl.ds(s0, sk), p * lanes:(p + 1) * lanes], qam_ref[h],
                         preferred_element_type=F32) + mask
            if bias_row0 is not None:
                lg = lg + nb_ref[h, bias_row0:bias_row0 + sk, :]
            lg_ref[slot, h] = lg
            bmax_ref[slot, h, 0:1, :] = jnp.max(lg, axis=0, keepdims=True)

    def accumulate(kb, slot):
        s0 = pl.multiple_of(kb * sk, sk)
        for h in range(N_HEADS):
            hs = slice(h * hd, (h + 1) * hd)
            m_old = m_ref[h, 0:1, :]
            m_new = jnp.maximum(m_old, bmax_ref[slot, h, 0:1, :])
            alpha = jnp.exp(m_old - m_new)
            pr = jnp.exp(lg_ref[slot, h] - m_new)
            l_ref[h, 0:1, :] = alpha * l_ref[h, 0:1, :] + jnp.sum(pr, axis=0, keepdims=True)
            acc_ref[hs, :] = alpha * acc_ref[hs, :] + jnp.dot(
                vat_ref[0, hs, pl.ds(s0, sk)], pr.astype(BF16), preferred_element_type=F32)
            m_ref[h, 0:1, :] = m_new

    n_far = jnp.maximum(i - 1, 0)

    @pl.when(n_far > 0)
    def _():
        logits(0, 0, None)

    def far_body(kb, carry):
        logits(kb, kb % 2, None)
        accumulate(kb - 1, (kb - 1) % 2)
        return carry

    lax.fori_loop(1, n_far, far_body, 0)

    @pl.when(n_far > 0)
    def _():
        accumulate(n_far - 1, (n_far - 1) % 2)

    @pl.when(i > 0)
    def _():
        logits(i - 1, 0, 0)
        accumulate(i - 1, 0)

    logits(i, 1, sk)
    accumulate(i, 1)

    for h in range(N_HEADS):
        hs = slice(h * hd, (h + 1) * hd)
        acc_ref[hs, :] = acc_ref[hs, :] / l_ref[h, 0:1, :]
    o_ref[0] = acc_ref[...].T.astype(o_ref.dtype)


def _dsa_t(qit, wit, ki, qat, ka, vat, near_bias, tq):
    bsz, w, s = qat.shape
    topk = min(TOPK_MAX, s // 4)
    tspec = lambda n: pl.BlockSpec((1, n, tq), lambda b, i: (b, 0, i))
    resident = lambda shape: pl.BlockSpec((1,) + shape, lambda b, i: (b, 0, 0), pipeline_mode=pl.Buffered(1))
    return pl.pallas_call(
        functools.partial(_dsa_t_kernel, tq=tq, topk=topk),
        grid=(bsz, s // tq),
        in_specs=[tspec(w), tspec(WI_ROWS), resident((s, 2 * IDX_DIM)),
                  tspec(w), resident((s, w)), resident((w, s)),
                  _const_spec(near_bias.shape)],
        out_specs=pl.BlockSpec((1, tq, w), lambda b, i: (b, i, 0)),
        out_shape=jax.ShapeDtypeStruct((bsz, s, w), BF16),
        scratch_shapes=[pltpu.VMEM((s, tq), F32),
                        pltpu.VMEM((KEY_BITS, s // tq, SUBLANES, tq), jnp.int32),
                        pltpu.VMEM((s // tq, SUBLANES, tq), jnp.int32),
                        pltpu.VMEM((SUBLANES, tq), F32),
                        pltpu.VMEM((SUBLANES, tq), jnp.int32),
                        pltpu.VMEM((2 * HEAD_DIM, N_HEADS * tq), BF16),
                        pltpu.VMEM((N_HEADS, 2 * HEAD_DIM, tq), BF16),
                        pltpu.VMEM((w, tq), F32),
                        pltpu.VMEM((N_HEADS, SUBLANES, tq), F32),
                        pltpu.VMEM((N_HEADS, SUBLANES, tq), F32),
                        pltpu.VMEM((2, N_HEADS, tq, tq), F32),
                        pltpu.VMEM((2, N_HEADS, SUBLANES, tq), F32)],
        compiler_params=_cparams(("arbitrary", "arbitrary")),
        name="dsa",
    )(qit, wit, ki, qat, ka, vat, near_bias)


SB_TQ = 256
SB_PAIRS = 2
SB_DEAD = -104.0


def _sb_kernel(q_ref, kt_ref, v_ref, o_ref, *, tq, pairs):
    i = pl.program_id(2)
    lanes = V7X_LANES
    heads = 2 * pairs
    lo_half = lax.broadcasted_iota(jnp.int32, (tq, lanes), 1) < HEAD_DIM
    r_i = lax.broadcasted_iota(jnp.int32, (tq, tq), 0)
    c_i = lax.broadcasted_iota(jnp.int32, (tq, tq), 1)
    strict = c_i < r_i
    after = jnp.where(r_i > c_i, 1.0, 0.0).astype(BF16)
    after2 = jnp.concatenate([after, after], axis=0)
    qms = []
    for p in range(pairs):
        q = q_ref[0, :, p * lanes:(p + 1) * lanes].astype(F32)
        qms += [jnp.where(lo_half, q, 0.0).astype(BF16), jnp.where(lo_half, 0.0, q).astype(BF16)]

    def tile(kb, diag, state):
        s0 = pl.multiple_of(kb * tq, tq)
        carries, accs = [], []
        for hh in range(heads):
            p = hh // 2
            carry, acc = state[hh], state[heads + hh]
            z = jnp.dot(qms[hh], kt_ref[0, p * lanes:(p + 1) * lanes, pl.ds(s0, tq)], preferred_element_type=F32)
            sp = jnp.log(1.0 + jnp.exp(-jnp.abs(z)))
            log_beta = jnp.minimum(z, 0.0) - sp
            log_rest = -(jnp.maximum(z, 0.0) + sp)
            if diag:
                log_rest = jnp.where(strict, log_rest, 0.0)
            hi = log_rest.astype(BF16)
            lo = (log_rest - hi.astype(F32)).astype(BF16)
            suffix = jnp.dot(jnp.concatenate([hi, lo], axis=1), after2, preferred_element_type=F32)
            a = jnp.exp(log_beta + suffix + carry)
            if diag:
                a = jnp.where(strict, a, 0.0)
            accs.append(acc + jnp.dot(a.astype(BF16), v_ref[0, pl.ds(s0, tq), p * lanes:(p + 1) * lanes],
                                      preferred_element_type=F32))
            carries.append(carry + jnp.sum(log_rest, axis=1, keepdims=True))
        return tuple(carries) + tuple(accs)

    zc = jnp.zeros((tq, 1), F32)
    za = jnp.zeros((tq, lanes), F32)
    state = tile(i, True, (zc,) * heads + (za,) * heads)

    def alive(ls):
        kb, state = ls
        worst = state[0]
        for hh in range(1, heads):
            worst = jnp.maximum(worst, state[hh])
        return jnp.logical_and(kb >= 0, jnp.max(worst) >= SB_DEAD)

    def body(ls):
        kb, state = ls
        return kb - 1, tile(kb, False, state)

    _, state = lax.while_loop(alive, body, (i - 1, state))
    for p in range(pairs):
        o_ref[0, :, p * lanes:(p + 1) * lanes] = jnp.where(
            lo_half, state[heads + 2 * p], state[heads + 2 * p + 1]).astype(o_ref.dtype)


def _stick_breaking(qb, kbt, vb, tq=SB_TQ, pairs=SB_PAIRS):
    bsz, s, w = qb.shape
    lanes = V7X_LANES * pairs
    return pl.pallas_call(
        functools.partial(_sb_kernel, tq=tq, pairs=pairs),
        grid=(bsz, w // lanes, s // tq),
        in_specs=[pl.BlockSpec((1, tq, lanes), lambda b, p, i: (b, i, p)),
                  pl.BlockSpec((1, lanes, s), lambda b, p, i: (b, p, 0)),
                  pl.BlockSpec((1, s, lanes), lambda b, p, i: (b, 0, p))],
        out_specs=pl.BlockSpec((1, tq, lanes), lambda b, p, i: (b, i, p)),
        out_shape=jax.ShapeDtypeStruct((bsz, s, w), BF16),
        compiler_params=_cparams(("arbitrary", "arbitrary", "arbitrary")),
        name="stick_breaking",
    )(qb, kbt, vb)


PEER_HEADS = 8
N_KEYS = 128
PEER_HALF = 128
PEER_TOPK = 16
PEER_GROUPS = 2 * PEER_HEADS


def _mix_kernel(x_ref, ya_ref, yb_ref, ga_ref, gb_ref, mod_ref, g2_ref, wa_ref, wb_ref, wo_ref, wq_ref,
                sk_ref, x1_ref, hn2t_ref, st_ref):
    ma = jnp.dot(ya_ref[0], wa_ref[...], preferred_element_type=F32)
    mb = jnp.dot(yb_ref[0], wb_ref[...], preferred_element_type=F32)
    merged = jax.nn.sigmoid(ga_ref[0]) * ma + jax.nn.sigmoid(gb_ref[0]) * mb
    x1 = x_ref[0] + mod_ref[0, 2:3, :] * jnp.dot(merged.astype(BF16), wo_ref[...], preferred_element_type=F32)
    x1_ref[0] = x1
    h2 = _modulated_norm(x1, g2_ref[...], mod_ref[0, 3:4, :], mod_ref[0, 4:5, :])
    hn2t_ref[0] = h2.T.astype(BF16)
    qh = jnp.dot(h2.astype(BF16), wq_ref[...], preferred_element_type=F32).astype(BF16)
    nt = (((1,), (1,)), ((), ()))
    for g in range(PEER_GROUPS):
        st_ref[0, g * N_KEYS:(g + 1) * N_KEYS, :] = lax.dot_general(
            sk_ref[g], qh[:, g * PEER_HALF:(g + 1) * PEER_HALF], nt, preferred_element_type=F32)


def _mix(x, ya, yb, ga, gb, mod, g2, w_proj_a, w_proj_b, w_out, peer_wq, sub_keys, tm=512):
    bsz, s, d = x.shape
    w = ya.shape[-1]
    nq = peer_wq.shape[1]
    sk = sub_keys.reshape(PEER_GROUPS, N_KEYS, PEER_HALF).astype(BF16)
    rspec = lambda n: pl.BlockSpec((1, tm, n), lambda b, i: (b, i, 0))
    tspec = lambda n: pl.BlockSpec((1, n, tm), lambda b, i: (b, 0, i))
    return pl.pallas_call(
        _mix_kernel,
        grid=(bsz, s // tm),
        in_specs=[rspec(d), rspec(w), rspec(w), rspec(d), rspec(d),
                  pl.BlockSpec((1, 6, d), lambda b, i: (b, 0, 0)), _const_spec((1, d)),
                  _const_spec((w, d)), _const_spec((w, d)), _const_spec((d, d)), _const_spec((d, nq)),
                  _const_spec(sk.shape)],
        out_specs=[rspec(d), tspec(d), tspec(PEER_GROUPS * N_KEYS)],
        out_shape=[jax.ShapeDtypeStruct((bsz, s, d), F32),
                   jax.ShapeDtypeStruct((bsz, d, s), BF16),
                   jax.ShapeDtypeStruct((bsz, PEER_GROUPS * N_KEYS, s), F32)],
        compiler_params=_cparams(("parallel", "parallel")),
        name="mix_out",
    )(x, ya, yb, ga, gb, mod, g2.reshape(1, d), w_proj_a.astype(BF16), w_proj_b.astype(BF16),
      w_out.astype(BF16), peer_wq.astype(BF16), sk)


N_STATS = 8
SELECT_WAYS = 4
GATE_DTYPE = BF16
GATE_LANES = 512
PEER_CAND_ROWS = -(-sum(PEER_TOPK // (k + 1) for k in range(PEER_TOPK)) // 8) * 8


def _peer_select_kernel(st_ref, stats_ref, tops_ref, cand_ref):
    tn = st_ref.shape[-1]
    lanes = V7X_LANES
    top_ref = tops_ref.at[0]

    def chunk_body(c, carry):
        cs = pl.ds(pl.multiple_of(c * lanes, lanes), lanes)

        def groups_body(gq, carry):
            gs = [gq * SELECT_WAYS + w for w in range(SELECT_WAYS)]
            sc = [st_ref[0, pl.ds(pl.multiple_of(g * N_KEYS, N_KEYS), N_KEYS), cs] for g in gs]
            for k in range(PEER_TOPK):
                for w, g in enumerate(gs):
                    m = jnp.max(sc[w], axis=0, keepdims=True)
                    top_ref[g, k:k + 1, cs] = m
                    sc[w] = jnp.where(sc[w] == m, -jnp.inf, sc[w])
            return carry

        lax.fori_loop(0, PEER_GROUPS // SELECT_WAYS, groups_body, 0)

        def heads_body(hq, carry):
            hs = [hq * 2, hq * 2 + 1]
            v1 = [top_ref[2 * h, :, cs] for h in hs]
            v2 = [top_ref[2 * h + 1, :, cs] for h in hs]
            cand = []
            for w in range(2):
                cand_ref[w] = jnp.full(cand_ref.shape[1:], -jnp.inf, F32)
                row = 0
                for k in range(PEER_TOPK):
                    n = PEER_TOPK // (k + 1)
                    cand_ref[w, row:row + n, :] = v1[w][k:k + 1, :] + v2[w][0:n, :]
                    row += n
                cand.append(cand_ref[w])
            sc = list(cand)
            for k in range(PEER_TOPK):
                mk = [jnp.max(sc[w], axis=0, keepdims=True) for w in range(2)]
                sc = [jnp.where(sc[w] == mk[w], -jnp.inf, sc[w]) for w in range(2)]
            for w, h in enumerate(hs):
                thr = mk[w]
                m = v1[w][0:1, :] + v2[w][0:1, :]
                z = jnp.sum(jnp.where(cand[w] >= thr, jnp.exp(cand[w] - m), 0.0), axis=0, keepdims=True)
                rows = [v1[w][PEER_TOPK - 1:PEER_TOPK, :], v2[w][PEER_TOPK - 1:PEER_TOPK, :], thr,
                        v1[w][0:1, :], v2[w][0:1, :], 1.0 / z, jnp.zeros((2, lanes), F32)]
                stats_ref[0, pl.ds(pl.multiple_of(h * N_STATS, N_STATS), N_STATS), cs] = jnp.concatenate(rows, axis=0)
            return carry

        lax.fori_loop(0, PEER_HEADS // 2, heads_body, 0)
        return carry

    lax.fori_loop(0, tn // lanes, chunk_body, 0)


def _peer_select(st, tn=512):
    bsz, rows, s = st.shape
    return pl.pallas_call(
        _peer_select_kernel,
        grid=(bsz, s // tn),
        in_specs=[pl.BlockSpec((1, rows, tn), lambda b, i: (b, 0, i))],
        out_specs=[pl.BlockSpec((1, PEER_HEADS * N_STATS, tn), lambda b, i: (b, 0, i)),
                   pl.BlockSpec((1, PEER_GROUPS, PEER_TOPK, tn), lambda b, i: (b, 0, 0, i))],
        out_shape=[jax.ShapeDtypeStruct((bsz, PEER_HEADS * N_STATS, s), F32),
                   jax.ShapeDtypeStruct((bsz, PEER_GROUPS, PEER_TOPK, s), F32)],
        scratch_shapes=[pltpu.VMEM((2, PEER_CAND_ROWS, V7X_LANES), F32)],
        compiler_params=_cparams(("parallel", "parallel")),
        name="peer_select",
    )(st)


def _peer_experts_kernel(ht_ref, st_ref, stats_ref, tops_ref, u_ref, vt_ref, o_ref,
                         len1_ref, e1_ref, rank2_ref, e2_ref, pre_ref, coef_ref, *, te, n_chunks):
    j = pl.program_id(2)
    tm = ht_ref.shape[-1]
    lanes = V7X_LANES
    rows_per_step = te // N_KEYS

    @pl.when(j == 0)
    def _():
        o_ref[...] = jnp.zeros(o_ref.shape, F32)
        coef_ref[1] = jnp.zeros(coef_ref.shape[1:], BF16)
        for h in range(PEER_HEADS):
            st = lambda k: stats_ref[0, h * N_STATS + k:h * N_STATS + k + 1, :]
            s1 = st_ref[0, (2 * h) * N_KEYS:(2 * h + 1) * N_KEYS, :]
            s2 = st_ref[0, (2 * h + 1) * N_KEYS:(2 * h + 2) * N_KEYS, :]
            v2 = tops_ref[0, 2 * h + 1]
            in1 = s1 >= st(0)
            in2 = s2 >= st(1)
            a1 = jnp.where(in1, s1, NEG)
            len1 = jnp.zeros(s1.shape, F32)
            rank2 = jnp.zeros(s2.shape, F32)
            for l in range(PEER_TOPK):
                len1 = len1 + jnp.where(a1 + v2[l:l + 1, :] >= st(2), 1.0, 0.0)
                rank2 = rank2 + jnp.where(v2[l:l + 1, :] > s2, 1.0, 0.0)
            len1_ref[h] = len1
            e1_ref[h] = jnp.where(in1, jnp.exp(s1 - st(3)) * st(5), 0.0)
            rank2_ref[h] = jnp.where(in2, rank2, float(PEER_TOPK)).astype(GATE_DTYPE)
            e2_ref[h] = jnp.where(in2, jnp.exp(s2 - st(4)), 0.0).astype(GATE_DTYPE)

    def add_previous_chunk():
        o_ref[0] += jnp.dot(vt_ref[...], coef_ref[(j + 1) % 2], preferred_element_type=F32)

    @pl.when(j < n_chunks)
    def _():
        ht = ht_ref[0]
        for r in range(rows_per_step):
            rs = slice(r * N_KEYS, (r + 1) * N_KEYS)
            pre_ref[rs, :] = jnp.dot(u_ref[rs, :], ht, preferred_element_type=F32)
        add_previous_chunk()
        first = pl.multiple_of(j * rows_per_step, rows_per_step)
        slot = j % 2
        zero = jnp.zeros((N_KEYS, GATE_LANES), GATE_DTYPE)
        for c in range(tm // GATE_LANES):
            cs = slice(c * GATE_LANES, (c + 1) * GATE_LANES)
            for r in range(rows_per_step):
                rs = slice(r * N_KEYS, (r + 1) * N_KEYS)
                gate = zero
                for h in range(PEER_HEADS):
                    row = lambda ref: jnp.broadcast_to(ref[h, pl.ds(first, rows_per_step), cs][r:r + 1, :],
                                                       (N_KEYS, GATE_LANES)).astype(GATE_DTYPE)
                    gate = gate + jnp.where(rank2_ref[h, :, cs] < row(len1_ref), row(e1_ref) * e2_ref[h, :, cs], zero)
                x = pre_ref[rs, cs]
                gelu = 0.5 * x * (1.0 + lax.erf(x * np.float32(math.sqrt(0.5))))
                coef_ref[slot, rs, cs] = (gate * gelu.astype(GATE_DTYPE)).astype(BF16)

    @pl.when(j == n_chunks)
    def _():
        add_previous_chunk()


def _peer_experts(hn2t, st, stats, tops, peer_u, peer_v, tm=512, te=2048):
    bsz, d, s = hn2t.shape
    n_exp = peer_u.shape[0]
    n_chunks = n_exp // te
    u = peer_u.astype(BF16)
    vt = peer_v.T.astype(BF16)
    tok = lambda n: pl.BlockSpec((1, n, tm), lambda b, i, j: (b, 0, i))
    table = lambda dt: pltpu.VMEM((PEER_HEADS, N_KEYS, tm), dt)
    return pl.pallas_call(
        functools.partial(_peer_experts_kernel, te=te, n_chunks=n_chunks),
        grid=(bsz, s // tm, n_chunks + 1),
        in_specs=[tok(d), tok(st.shape[1]), tok(stats.shape[1]),
                  pl.BlockSpec((1, PEER_GROUPS, PEER_TOPK, tm), lambda b, i, j: (b, 0, 0, i)),
                  pl.BlockSpec((te, d), lambda b, i, j: (jnp.minimum(j, n_chunks - 1), 0)),
                  pl.BlockSpec((d, te), lambda b, i, j: (0, jnp.maximum(j - 1, 0)))],
        out_specs=tok(d),
        out_shape=jax.ShapeDtypeStruct((bsz, d, s), F32),
        scratch_shapes=[table(F32), table(F32), table(GATE_DTYPE), table(GATE_DTYPE),
                        pltpu.VMEM((te, tm), F32), pltpu.VMEM((2, te, tm), BF16)],
        compiler_params=_cparams(("parallel", "parallel", "arbitrary")),
        name="peer_experts",
    )(hn2t, st, stats, tops, u, vt)


def _final_kernel(x1_ref, ft_ref, mod_ref, g_ref, o_ref):
    x2 = x1_ref[0] + mod_ref[0, 5:6, :] * ft_ref[0].T
    y = x2 * lax.rsqrt(jnp.mean(x2 * x2, axis=-1, keepdims=True) + EPS)
    o_ref[0] = y * g_ref[...]


def _final(x1, ft, mod, final_g, tm=512):
    bsz, s, d = x1.shape
    return pl.pallas_call(
        _final_kernel,
        grid=(bsz, s // tm),
        in_specs=[pl.BlockSpec((1, tm, d), lambda b, i: (b, i, 0)),
                  pl.BlockSpec((1, d, tm), lambda b, i: (b, 0, i)),
                  pl.BlockSpec((1, 6, d), lambda b, i: (b, 0, 0)), _const_spec((1, d))],
        out_specs=pl.BlockSpec((1, tm, d), lambda b, i: (b, i, 0)),
        out_shape=jax.ShapeDtypeStruct((bsz, s, d), F32),
        compiler_params=_cparams(("parallel", "parallel")),
        name="final_norm",
    )(x1, ft, mod, final_g.reshape(1, d))


def kernel(x, c, w_ada, b_ada, norm1_g, norm2_g, w_in, rel_bias, w_proj_a, w_proj_b, w_out,
           peer_wq, peer_sub_keys, peer_u, peer_v, final_g):
    bsz, s, d = x.shape
    mod = _ada(c, w_ada[0], b_ada[0]).reshape(bsz, 6, d)
    ka, ki, qb, vb, ga, gb, qat, vat, qit, wit, kbt = _in_proj(x, mod, norm1_g[0], w_in[0])
    ya = _dsa_t(qit, wit, ki, qat, ka, vat, _near_bias(rel_bias, DSA_T_TQ), DSA_T_TQ)
    yb = _stick_breaking(qb, kbt, vb)
    x1, hn2t, st = _mix(x, ya, yb, ga, gb, mod, norm2_g[0], w_proj_a[0], w_proj_b[0], w_out[0],
                        peer_wq[0], peer_sub_keys[0])
    stats, tops = _peer_select(st)
    ft = _peer_experts(hn2t, st, stats, tops, peer_u[0], peer_v[0])
    return _final(x1, ft, mod, final_g)
```

```python
import functools
import math

import jax
import jax.numpy as jnp
import numpy as np
from jax import lax
from jax.experimental import pallas as pl
from jax.experimental.pallas import tpu as pltpu

N_HEADS = 8
HEAD_DIM = 64
WIDTH = N_HEADS * HEAD_DIM
IDX_DIM = 64
TOPK_MAX = 256
N_BUCKETS = 32
MAX_DISTANCE = 128
EPS = 1e-6

V7X_LANES = 128
V7X_VMEM_LIMIT = 56 * 1024 * 1024

F32 = jnp.float32
BF16 = jnp.bfloat16


def _cparams(sem):
    return pltpu.CompilerParams(dimension_semantics=sem, vmem_limit_bytes=V7X_VMEM_LIMIT)


def _const_spec(shape):
    return pl.BlockSpec(shape, lambda *_: (0,) * len(shape), pipeline_mode=pl.Buffered(1))


def _ada_kernel(c_ref, w_ref, b_ref, o_ref):
    c = c_ref[...]
    cond = c * jax.nn.sigmoid(c)
    o_ref[...] = jnp.dot(cond, w_ref[...], preferred_element_type=F32,
                         precision=lax.Precision.HIGHEST) + b_ref[...]


def _ada(c, w, b):
    bsz, d = c.shape
    n = w.shape[1]
    tn = 1536
    return pl.pallas_call(
        _ada_kernel,
        grid=(n // tn,),
        in_specs=[pl.BlockSpec((bsz, d), lambda j: (0, 0)),
                  pl.BlockSpec((d, tn), lambda j: (0, j)),
                  pl.BlockSpec((1, tn), lambda j: (0, j))],
        out_specs=pl.BlockSpec((bsz, tn), lambda j: (0, j)),
        out_shape=jax.ShapeDtypeStruct((bsz, n), F32),
        compiler_params=_cparams(("arbitrary",)),
        name="ada",
    )(c, w, b.reshape(1, n))


def _modulated_norm(x, g, shift, scale):
    y = x * lax.rsqrt(jnp.mean(x * x, axis=-1, keepdims=True) + EPS)
    return (y * g) * (1.0 + scale) + shift


WI_ROWS = 16


def _in_proj_kernel(x_ref, mod_ref, g_ref, wr_ref, wt_ref,
                    ka_ref, ki_ref, qb_ref, vb_ref, ga_ref, gb_ref,
                    qat_ref, vat_ref, qit_ref, wit_ref, kbt_ref):
    h = _modulated_norm(x_ref[0], g_ref[...], mod_ref[0, 0:1, :], mod_ref[0, 1:2, :]).astype(BF16)
    c0 = 0
    for ref in (ka_ref, ki_ref, qb_ref, vb_ref, ga_ref, gb_ref):
        n = ref.shape[-1]
        for part in range(0, n, WIDTH):
            pw = min(WIDTH, n - part)
            ref[0, :, part:part + pw] = jnp.dot(h, wr_ref[:, c0 + part:c0 + part + pw],
                                               preferred_element_type=F32).astype(ref.dtype)
        c0 += n
    nt = (((1,), (1,)), ((), ()))
    r0 = 0
    for ref in (qat_ref, vat_ref, qit_ref, wit_ref, kbt_ref):
        n = ref.shape[1]
        ref[0] = lax.dot_general(wt_ref[r0:r0 + n, :], h, nt, preferred_element_type=F32).astype(ref.dtype)
        r0 += n


def _in_proj(x, mod, g1, w_in, tm=512):
    bsz, s, d = x.shape
    w = WIDTH
    offs = np.cumsum([0, w, w, w, w, IDX_DIM, N_HEADS, w, w, w, d, d])
    col = lambda i: w_in[:, offs[i]:offs[i + 1]]
    qscale = HEAD_DIM ** -0.5
    w_rows = jnp.concatenate([col(1), col(4), col(4), col(6) * qscale, col(8), col(9), col(10)],
                             axis=1).astype(BF16)
    w_t = jnp.concatenate([col(0) * qscale, col(2), col(3),
                           jnp.pad(col(5), ((0, 0), (0, WI_ROWS - N_HEADS))), col(7)], axis=1).T.astype(BF16)

    row = lambda n, dt: jax.ShapeDtypeStruct((bsz, s, n), dt)
    tr = lambda n, dt: jax.ShapeDtypeStruct((bsz, n, s), dt)
    rspec = lambda n: pl.BlockSpec((1, tm, n), lambda b, i: (b, i, 0))
    tspec = lambda n: pl.BlockSpec((1, n, tm), lambda b, i: (b, 0, i))
    return pl.pallas_call(
        _in_proj_kernel,
        grid=(bsz, s // tm),
        in_specs=[rspec(d),
                  pl.BlockSpec((1, 6, d), lambda b, i: (b, 0, 0)),
                  _const_spec((1, d)),
                  _const_spec(w_rows.shape), _const_spec(w_t.shape)],
        out_specs=[rspec(w), rspec(2 * IDX_DIM), rspec(w), rspec(w), rspec(d), rspec(d),
                   tspec(w), tspec(w), tspec(w), tspec(WI_ROWS), tspec(w)],
        out_shape=[row(w, BF16), row(2 * IDX_DIM, BF16), row(w, BF16), row(w, BF16), row(d, F32), row(d, F32),
                   tr(w, BF16), tr(w, BF16), tr(w, BF16), tr(WI_ROWS, F32), tr(w, BF16)],
        compiler_params=_cparams(("parallel", "parallel")),
        name="in_proj",
    )(x, mod, g1.reshape(1, d), w_rows, w_t)


def _t5_bucket_np(n):
    max_exact = N_BUCKETS // 2
    nf = np.maximum(n, 1).astype(np.float32)
    large = max_exact + (np.log(nf / np.float32(max_exact)) / np.float32(math.log(MAX_DISTANCE / max_exact))
                         * np.float32(N_BUCKETS - max_exact)).astype(np.int32)
    large = np.minimum(large, N_BUCKETS - 1)
    return np.where(n < max_exact, n, large).astype(np.int32)


def _near_bias_kernel(bk_ref, rb_ref, o_ref):
    bk = bk_ref[...]
    for h in range(N_HEADS):
        far = rb_ref[N_BUCKETS - 1, h]
        acc = jnp.zeros(bk.shape, F32)
        for b in range(N_BUCKETS - 1):
            acc = jnp.where(bk == b, rb_ref[b, h] - far, acc)
        o_ref[h] = acc


def _near_bias(rel_bias, tq):
    a = np.arange(2 * tq)[:, None]
    b = np.arange(tq)[None, :]
    n = tq + b - a
    far_from = int(np.argmax(_t5_bucket_np(np.arange(4 * MAX_DISTANCE)) == N_BUCKETS - 1))
    assert far_from <= tq, "distances beyond the near zone must share the last bucket"
    bucket = np.where(n >= 0, _t5_bucket_np(np.maximum(n, 0)), N_BUCKETS - 1).astype(np.int32)
    return pl.pallas_call(
        _near_bias_kernel,
        in_specs=[pl.BlockSpec(memory_space=pltpu.VMEM), pl.BlockSpec(memory_space=pltpu.SMEM)],
        out_specs=pl.BlockSpec(memory_space=pltpu.VMEM),
        out_shape=jax.ShapeDtypeStruct((N_HEADS, 2 * tq, tq), F32),
        name="near_bias",
    )(jnp.asarray(bucket), rel_bias)


INT_MIN = -2 ** 31
NEG = -1e30


def _sortable(x):
    bits = pltpu.bitcast(x, jnp.int32)
    return bits ^ ((bits >> 31) & 0x7FFFFFFF)


DSA_T_TQ = 256
SUBLANES = 8
KEY_BITS = 32
IDX_SCALE = (IDX_DIM ** -0.5) * (N_HEADS ** -0.5)
assert DSA_T_TQ == KEY_BITS * SUBLANES, "one key block must fill the 32 bit positions of a sublane tile"


def _bit_transpose(rows):
    a = list(rows)
    j, m = KEY_BITS // 2, 0x0000FFFF
    while j:
        for k in range(KEY_BITS):
            if k & j:
                continue
            t = (a[k] ^ lax.shift_right_logical(a[k + j], jnp.int32(j))) & m
            a[k] = a[k] ^ t
            a[k + j] = a[k + j] ^ lax.shift_left(t, jnp.int32(j))
        j >>= 1
        m ^= m << j
    return a


def _dsa_t_kernel(qit_ref, wit_ref, ki_ref, qat_ref, ka_ref, vat_ref, nb_ref, o_ref,
                  sc_ref, planes_ref, alive_ref, thr_ref, cnt_ref, qim_ref, qam_ref, acc_ref, m_ref, l_ref,
                  lg_ref, bmax_ref,
                  *, tq, topk):
    i = pl.program_id(1)
    t0 = i * tq
    lanes = V7X_LANES
    sk = tq
    hd = HEAD_DIM
    idx_bits = (sc_ref.shape[0] - 1).bit_length()

    zeros = jnp.zeros((hd, tq), BF16)
    for h in range(N_HEADS):
        own = slice((h % 2) * hd, (h % 2 + 1) * hd)
        other = slice((1 - h % 2) * hd, (2 - h % 2) * hd)
        qim_ref[own, h * tq:(h + 1) * tq] = qit_ref[0, h * hd:(h + 1) * hd, :]
        qim_ref[other, h * tq:(h + 1) * tq] = zeros
        qam_ref[h, own, :] = qat_ref[0, h * hd:(h + 1) * hd, :]
        qam_ref[h, other, :] = zeros

    n_blocks = i + 1
    key_i = lax.broadcasted_iota(jnp.int32, (sk, tq), 0)
    qry_i = t0 + lax.broadcasted_iota(jnp.int32, (sk, tq), 1)
    w = wit_ref[0]

    def score_body(kb, carry):
        s0 = pl.multiple_of(kb * sk, sk)
        z = jnp.dot(ki_ref[0, pl.ds(s0, sk), :], qim_ref[...], preferred_element_type=F32)
        sc = w[0:1, :] * jnp.maximum(z[:, 0:tq], 0.0)
        for h in range(1, N_HEADS):
            sc = sc + w[h:h + 1, :] * jnp.maximum(z[:, h * tq:(h + 1) * tq], 0.0)
        sc = jnp.where(key_i + s0 <= qry_i, sc * IDX_SCALE, -jnp.inf)
        sc_ref[pl.ds(s0, sk), :] = sc
        ordered = _sortable(sc) ^ INT_MIN
        planes = _bit_transpose([ordered[g * SUBLANES:(g + 1) * SUBLANES, :] for g in range(KEY_BITS)])
        for p in range(KEY_BITS):
            planes_ref[p, kb] = planes[p]
        return carry

    lax.fori_loop(0, n_blocks, score_body, 0)

    def first_count(kb, cnt):
        alive_ref[kb] = jnp.full((SUBLANES, tq), -1, jnp.int32)
        return cnt + lax.population_count(planes_ref[0, kb])

    zero_cnt = jnp.zeros((SUBLANES, tq), jnp.int32)
    cnt0 = lax.fori_loop(0, n_blocks, first_count, zero_cnt)

    def bit_body(p, carry):
        r, k_left, cnt = carry
        total = jnp.sum(cnt, axis=0, keepdims=True)
        take = total >= k_left
        r = jnp.where(take, r | lax.shift_left(jnp.int32(1), KEY_BITS - 1 - p), r)
        k_left = jnp.where(take, k_left, k_left - total)
        drop = jnp.where(take, 0, -1)
        p_next = jnp.minimum(p + 1, KEY_BITS - 1)

        def narrow(kb, c):
            alive = alive_ref[kb] & (planes_ref[p, kb] ^ drop)
            alive_ref[kb] = alive
            return c + lax.population_count(alive & planes_ref[p_next, kb])

        return r, k_left, lax.fori_loop(0, n_blocks, narrow, zero_cnt)

    r, k_left, _ = lax.fori_loop(0, KEY_BITS, bit_body,
                                 (jnp.zeros((1, tq), jnp.int32), jnp.full((1, tq), topk, jnp.int32), cnt0))
    guess_bits = r ^ INT_MIN
    guess = pltpu.bitcast(guess_bits ^ ((guess_bits >> 31) & 0x7FFFFFFF), F32)

    def count_where(pred):
        def body(kb, c):
            hit = jnp.where(pred(sc_ref[pl.ds(pl.multiple_of(kb * sk, sk), sk), :]), 1, 0)
            for g in range(sk // SUBLANES):
                c = c + hit[g * SUBLANES:(g + 1) * SUBLANES, :]
            return c
        return jnp.sum(lax.fori_loop(0, n_blocks, body, zero_cnt), axis=0, keepdims=True)

    few = qry_i[0:1, :] + 1 < topk
    lowest = float(jnp.finfo(F32).min)
    n_ge = count_where(lambda s: s >= guess)
    n_gt = count_where(lambda s: s > guess)
    exact =jnp.logical_or(few, jnp.logical_and(n_ge >= topk, n_gt < topk))
    thr_ref[0:1, :] = jnp.where(few, lowest, guess)
    cnt_ref[0:1, :] = n_ge
    cnt_ref[1:2, :] = n_gt

    @pl.when(jnp.min(jnp.where(exact, 1, 0)) == 0)
    def _():
        def extreme(kb, c):
            blk = sc_ref[pl.ds(pl.multiple_of(kb * sk, sk), sk), :]
            return (jnp.minimum(c[0], jnp.min(jnp.where(blk > -jnp.inf, blk, jnp.inf), axis=0, keepdims=True)),
                    jnp.maximum(c[1], jnp.max(blk, axis=0, keepdims=True)))

        lo, hi = lax.fori_loop(0, n_blocks, extreme,
                               (jnp.full((1, tq), jnp.inf, F32), jnp.full((1, tq), -jnp.inf, F32)))
        top = hi

        def unsettled(state):
            lo, hi = state
            mid = 0.5 * lo + 0.5 * hi
            open_ = jnp.logical_and(jnp.logical_not(exact), jnp.logical_and(mid > lo, mid < hi))
            return jnp.max(jnp.where(open_, 1, 0)) > 0

        def halve(state):
            lo, hi = state
            mid = 0.5 * lo + 0.5 * hi
            up = count_where(lambda s: s >= mid) >= topk
            return jnp.where(up, mid, lo), jnp.where(up, hi, mid)

        lo, hi = lax.while_loop(unsettled, halve, (lo, hi))
        found = jnp.where(count_where(lambda s: s >= top) >= topk, top, lo)
        thr_new = jnp.where(exact, thr_ref[0:1, :], found)
        thr_ref[0:1, :] = thr_new
        cnt_ref[0:1, :] = count_where(lambda s: s >= thr_new)
        cnt_ref[1:2, :] = count_where(lambda s: s > thr_new)

    thr = thr_ref[0:1, :]

    keep_tied = topk - cnt_ref[1:2, :]
    surplus = jnp.logical_and(cnt_ref[0:1, :] > topk, jnp.logical_not(few))

    @pl.when(jnp.max(jnp.where(surplus, 1, 0)) > 0)
    def _():
        def tied_before(cand):
            def body(kb, c):
                s0 = pl.multiple_of(kb * sk, sk)
                hit = jnp.where(sc_ref[pl.ds(s0, sk), :] == thr, jnp.where(key_i + s0 < cand, 1, 0), 0)
                for g in range(sk // SUBLANES):
                    c = c + hit[g * SUBLANES:(g + 1) * SUBLANES, :]
                return c
            return jnp.sum(lax.fori_loop(0, n_blocks, body, zero_cnt), axis=0, keepdims=True)

        def index_bit(it, c):
            cand = c | lax.shift_left(jnp.int32(1), idx_bits - 1 - it)
            return jnp.where(tied_before(cand) < keep_tied, cand, c)

        last_kept = lax.fori_loop(0, idx_bits, index_bit, jnp.zeros((1, tq), jnp.int32))

        def remove(kb, carry):
            s0 = pl.multiple_of(kb * sk, sk)
            sc = sc_ref[pl.ds(s0, sk), :]
            drop = jnp.where(sc == thr, jnp.where(jnp.logical_and(key_i + s0 > last_kept, surplus), 1, 0), 0)
            sc_ref[pl.ds(s0, sk), :] = jnp.where(drop > 0, -jnp.inf, sc)
            return carry

        lax.fori_loop(0, n_blocks, remove, 0)

    m_ref[...] = jnp.full(m_ref.shape, NEG, F32)
    l_ref[...] = jnp.zeros(l_ref.shape, F32)
    acc_ref[...] = jnp.zeros(acc_ref.shape, F32)

    def logits(kb, slot, bias_row0):
        s0 = pl.multiple_of(kb * sk, sk)
        mask = jnp.where(sc_ref[pl.ds(s0, sk), :] >= thr, 0.0, NEG)
        for h in range(N_HEADS):
            p = h // 2
            lg = jnp.dot(ka_ref[0, pl.ds(s0, sk), p * lanes:(p + 1) * lanes], qam_ref[h],
                         preferred_element_type=F32) + mask
            if bias_row0 is not None:
                lg = lg + nb_ref[h, bias_row0:bias_row0 + sk, :]
            lg_ref[slot, h] = lg
            bmax_ref[slot, h, 0:1, :] = jnp.max(lg, axis=0, keepdims=True)

    def accumulate(kb, slot):
        s0 = pl.multiple_of(kb * sk, sk)
        for h in range(N_HEADS):
            hs = slice(h * hd, (h + 1) * hd)
            m_old = m_ref[h, 0:1, :]
            m_new = jnp.maximum(m_old, bmax_ref[slot, h, 0:1, :])
            alpha = jnp.exp(m_old - m_new)
            pr = jnp.exp(lg_ref[slot, h] - m_new)
            l_ref[h, 0:1, :] = alpha * l_ref[h, 0:1, :] + jnp.sum(pr, axis=0, keepdims=True)
            acc_ref[hs, :] = alpha * acc_ref[hs, :] + jnp.dot(
                vat_ref[0, hs, pl.ds(s0, sk)], pr.astype(BF16), preferred_element_type=F32)
            m_ref[h, 0:1, :] = m_new

    n_far = jnp.maximum(i - 1, 0)

    @pl.when(n_far > 0)
    def _():
        logits(0, 0, None)

    def far_body(kb, carry):
        logits(kb, kb % 2, None)
        accumulate(kb - 1, (kb - 1) % 2)
        return carry

    lax.fori_loop(1, n_far, far_body, 0)

    @pl.when(n_far > 0)
    def _():
        accumulate(n_far - 1, (n_far - 1) % 2)

    @pl.when(i > 0)
    def _():
        logits(i - 1, 0, 0)
        accumulate(i - 1, 0)

    logits(i, 1, sk)
    accumulate(i, 1)

    for h in range(N_HEADS):
        hs = slice(h * hd, (h + 1) * hd)
        acc_ref[hs, :] = acc_ref[hs, :] / l_ref[h, 0:1, :]
    o_ref[0] = acc_ref[...].T.astype(o_ref.dtype)


def _dsa_t(qit, wit, ki, qat, ka, vat, near_bias, tq):
    bsz, w, s = qat.shape
    topk = min(TOPK_MAX, s // 4)
    tspec = lambda n: pl.BlockSpec((1, n, tq), lambda b, i: (b, 0, i))
    resident = lambda shape: pl.BlockSpec((1,) + shape, lambda b, i: (b, 0, 0), pipeline_mode=pl.Buffered(1))
    return pl.pallas_call(
        functools.partial(_dsa_t_kernel, tq=tq, topk=topk),
        grid=(bsz, s // tq),
        in_specs=[tspec(w), tspec(WI_ROWS), resident((s, 2 * IDX_DIM)),
                  tspec(w), resident((s, w)), resident((w, s)),
                  _const_spec(near_bias.shape)],
        out_specs=pl.BlockSpec((1, tq, w), lambda b, i: (b, i, 0)),
        out_shape=jax.ShapeDtypeStruct((bsz, s, w), BF16),
        scratch_shapes=[pltpu.VMEM((s, tq), F32),
                        pltpu.VMEM((KEY_BITS, s // tq, SUBLANES, tq), jnp.int32),
                        pltpu.VMEM((s // tq, SUBLANES, tq), jnp.int32),
                        pltpu.VMEM((SUBLANES, tq), F32),
                        pltpu.VMEM((SUBLANES, tq), jnp.int32),
                        pltpu.VMEM((2 * HEAD_DIM, N_HEADS * tq), BF16),
                        pltpu.VMEM((N_HEADS, 2 * HEAD_DIM, tq), BF16),
                        pltpu.VMEM((w, tq), F32),
                        pltpu.VMEM((N_HEADS, SUBLANES, tq), F32),
                        pltpu.VMEM((N_HEADS, SUBLANES, tq), F32),
                        pltpu.VMEM((2, N_HEADS, tq, tq), F32),
                        pltpu.VMEM((2, N_HEADS, SUBLANES, tq), F32)],
        compiler_params=_cparams(("arbitrary", "arbitrary")),
        name="dsa",
    )(qit, wit, ki, qat, ka, vat, near_bias)


SB_TQ = 256
SB_PAIRS = 2
SB_DEAD = -104.0


def _sb_kernel(q_ref, kt_ref, v_ref, o_ref, *, tq, pairs):
    i = pl.program_id(2)
    lanes = V7X_LANES
    heads = 2 * pairs
    lo_half = lax.broadcasted_iota(jnp.int32, (tq, lanes), 1) < HEAD_DIM
    r_i = lax.broadcasted_iota(jnp.int32, (tq, tq), 0)
    c_i = lax.broadcasted_iota(jnp.int32, (tq, tq), 1)
    strict = c_i < r_i
    after = jnp.where(r_i > c_i, 1.0, 0.0).astype(BF16)
    after2 = jnp.concatenate([after, after], axis=0)
    qms = []
    for p in range(pairs):
        q = q_ref[0, :, p * lanes:(p + 1) * lanes].astype(F32)
        qms += [jnp.where(lo_half, q, 0.0).astype(BF16), jnp.where(lo_half, 0.0, q).astype(BF16)]

    def tile(kb, diag, state):
        s0 = pl.multiple_of(kb * tq, tq)
        carries, accs = [], []
        for hh in range(heads):
            p = hh // 2
            carry, acc = state[hh], state[heads + hh]
            z = jnp.dot(qms[hh], kt_ref[0, p * lanes:(p + 1) * lanes, pl.ds(s0, tq)], preferred_element_type=F32)
            sp = jnp.log(1.0 + jnp.exp(-jnp.abs(z)))
            log_beta = jnp.minimum(z, 0.0) - sp
            log_rest = -(jnp.maximum(z, 0.0) + sp)
            if diag:
                log_rest = jnp.where(strict, log_rest, 0.0)
            hi = log_rest.astype(BF16)
            lo = (log_rest - hi.astype(F32)).astype(BF16)
            suffix = jnp.dot(jnp.concatenate([hi, lo], axis=1), after2, preferred_element_type=F32)
            a = jnp.exp(log_beta + suffix + carry)
            if diag:
                a = jnp.where(strict, a, 0.0)
            accs.append(acc + jnp.dot(a.astype(BF16), v_ref[0, pl.ds(s0, tq), p * lanes:(p + 1) * lanes],
                                      preferred_element_type=F32))
            carries.append(carry + jnp.sum(log_rest, axis=1, keepdims=True))
        return tuple(carries) + tuple(accs)

    zc = jnp.zeros((tq, 1), F32)
    za = jnp.zeros((tq, lanes), F32)
    state = tile(i, True, (zc,) * heads + (za,) * heads)

    def alive(ls):
        kb, state = ls
        worst = state[0]
        for hh in range(1, heads):
            worst = jnp.maximum(worst, state[hh])
        return jnp.logical_and(kb >= 0, jnp.max(worst) >= SB_DEAD)

    def body(ls):
        kb, state = ls
        return kb - 1, tile(kb, False, state)

    _, state = lax.while_loop(alive, body, (i - 1, state))
    for p in range(pairs):
        o_ref[0, :, p * lanes:(p + 1) * lanes] = jnp.where(
            lo_half, state[heads + 2 * p], state[heads + 2 * p + 1]).astype(o_ref.dtype)


def _stick_breaking(qb, kbt, vb, tq=SB_TQ, pairs=SB_PAIRS):
    bsz, s, w = qb.shape
    lanes = V7X_LANES * pairs
    return pl.pallas_call(
        functools.partial(_sb_kernel, tq=tq, pairs=pairs),
        grid=(bsz, w // lanes, s // tq),
        in_specs=[pl.BlockSpec((1, tq, lanes), lambda b, p, i: (b, i, p)),
                  pl.BlockSpec((1, lanes, s), lambda b, p, i: (b, p, 0)),
                  pl.BlockSpec((1, s, lanes), lambda b, p, i: (b, 0, p))],
        out_specs=pl.BlockSpec((1, tq, lanes), lambda b, p, i: (b, i, p)),
        out_shape=jax.ShapeDtypeStruct((bsz, s, w), BF16),
        compiler_params=_cparams(("arbitrary", "arbitrary", "arbitrary")),
        name="stick_breaking",
    )(qb, kbt, vb)


PEER_HEADS = 8
N_KEYS = 128
PEER_HALF = 128
PEER_TOPK = 16
PEER_GROUPS = 2 * PEER_HEADS


def _mix_kernel(x_ref, ya_ref, yb_ref, ga_ref, gb_ref, mod_ref, g2_ref, wa_ref, wb_ref, wo_ref, wq_ref,
                sk_ref, x1_ref, hn2t_ref, st_ref):
    ma = jnp.dot(ya_ref[0], wa_ref[...], preferred_element_type=F32)
    mb = jnp.dot(yb_ref[0], wb_ref[...], preferred_element_type=F32)
    merged = jax.nn.sigmoid(ga_ref[0]) * ma + jax.nn.sigmoid(gb_ref[0]) * mb
    x1 = x_ref[0] + mod_ref[0, 2:3, :] * jnp.dot(merged.astype(BF16), wo_ref[...], preferred_element_type=F32)
    x1_ref[0] = x1
    h2 = _modulated_norm(x1, g2_ref[...], mod_ref[0, 3:4, :], mod_ref[0, 4:5, :])
    hn2t_ref[0] = h2.T.astype(BF16)
    qh = jnp.dot(h2.astype(BF16), wq_ref[...], preferred_element_type=F32).astype(BF16)
    nt = (((1,), (1,)), ((), ()))
    for g in range(PEER_GROUPS):
        st_ref[0, g * N_KEYS:(g + 1) * N_KEYS, :] = lax.dot_general(
            sk_ref[g], qh[:, g * PEER_HALF:(g + 1) * PEER_HALF], nt, preferred_element_type=F32)


def _mix(x, ya, yb, ga, gb, mod, g2, w_proj_a, w_proj_b, w_out, peer_wq, sub_keys, tm=512):
    bsz, s, d = x.shape
    w = ya.shape[-1]
    nq = peer_wq.shape[1]
    sk = sub_keys.reshape(PEER_GROUPS, N_KEYS, PEER_HALF).astype(BF16)
    rspec = lambda n: pl.BlockSpec((1, tm, n), lambda b, i: (b, i, 0))
    tspec = lambda n: pl.BlockSpec((1, n, tm), lambda b, i: (b, 0, i))
    return pl.pallas_call(
        _mix_kernel,
        grid=(bsz, s // tm),
        in_specs=[rspec(d), rspec(w), rspec(w), rspec(d), rspec(d),
                  pl.BlockSpec((1, 6, d), lambda b, i: (b, 0, 0)), _const_spec((1, d)),
                  _const_spec((w, d)), _const_spec((w, d)), _const_spec((d, d)), _const_spec((d, nq)),
                  _const_spec(sk.shape)],
        out_specs=[rspec(d), tspec(d), tspec(PEER_GROUPS * N_KEYS)],
        out_shape=[jax.ShapeDtypeStruct((bsz, s, d), F32),
                   jax.ShapeDtypeStruct((bsz, d, s), BF16),
                   jax.ShapeDtypeStruct((bsz, PEER_GROUPS * N_KEYS, s), F32)],
        compiler_params=_cparams(("parallel", "parallel")),
        name="mix_out",
    )(x, ya, yb, ga, gb, mod, g2.reshape(1, d), w_proj_a.astype(BF16), w_proj_b.astype(BF16),
      w_out.astype(BF16), peer_wq.astype(BF16), sk)


N_STATS = 8
SELECT_WAYS = 4
GATE_DTYPE = BF16
GATE_LANES = 512
PEER_CAND_ROWS = -(-sum(PEER_TOPK // (k + 1) for k in range(PEER_TOPK)) // 8) * 8


def _peer_select_kernel(st_ref, stats_ref, tops_ref, cand_ref):
    tn = st_ref.shape[-1]
    lanes = V7X_LANES
    top_ref = tops_ref.at[0]

    def chunk_body(c, carry):
        cs = pl.ds(pl.multiple_of(c * lanes, lanes), lanes)

        def groups_body(gq, carry):
            gs = [gq * SELECT_WAYS + w for w in range(SELECT_WAYS)]
            sc = [st_ref[0, pl.ds(pl.multiple_of(g * N_KEYS, N_KEYS), N_KEYS), cs] for g in gs]
            for k in range(PEER_TOPK):
                for w, g in enumerate(gs):
                    m = jnp.max(sc[w], axis=0, keepdims=True)
                    top_ref[g, k:k + 1, cs] = m
                    sc[w] = jnp.where(sc[w] == m, -jnp.inf, sc[w])
            return carry

        lax.fori_loop(0, PEER_GROUPS // SELECT_WAYS, groups_body, 0)

        def heads_body(hq, carry):
            hs = [hq * 2, hq * 2 + 1]
            v1 = [top_ref[2 * h, :, cs] for h in hs]
            v2 = [top_ref[2 * h + 1, :, cs] for h in hs]
            cand = []
            for w in range(2):
                cand_ref[w] = jnp.full(cand_ref.shape[1:], -jnp.inf, F32)
                row = 0
                for k in range(PEER_TOPK):
                    n = PEER_TOPK // (k + 1)
                    cand_ref[w, row:row + n, :] = v1[w][k:k + 1, :] + v2[w][0:n, :]
                    row += n
                cand.append(cand_ref[w])
            sc = list(cand)
            for k in range(PEER_TOPK):
                mk = [jnp.max(sc[w], axis=0, keepdims=True) for w in range(2)]
                sc = [jnp.where(sc[w] == mk[w], -jnp.inf, sc[w]) for w in range(2)]
            for w, h in enumerate(hs):
                thr = mk[w]
                m = v1[w][0:1, :] + v2[w][0:1, :]
                z = jnp.sum(jnp.where(cand[w] >= thr, jnp.exp(cand[w] - m), 0.0), axis=0, keepdims=True)
                rows = [v1[w][PEER_TOPK - 1:PEER_TOPK, :], v2[w][PEER_TOPK - 1:PEER_TOPK, :], thr,
                        v1[w][0:1, :], v2[w][0:1, :], 1.0 / z, jnp.zeros((2, lanes), F32)]
                stats_ref[0, pl.ds(pl.multiple_of(h * N_STATS, N_STATS), N_STATS), cs] = jnp.concatenate(rows, axis=0)
            return carry

        lax.fori_loop(0, PEER_HEADS // 2, heads_body, 0)
        return carry

    lax.fori_loop(0, tn // lanes, chunk_body, 0)


def _peer_select(st, tn=512):
    bsz, rows, s = st.shape
    return pl.pallas_call(
        _peer_select_kernel,
        grid=(bsz, s // tn),
        in_specs=[pl.BlockSpec((1, rows, tn), lambda b, i: (b, 0, i))],
        out_specs=[pl.BlockSpec((1, PEER_HEADS * N_STATS, tn), lambda b, i: (b, 0, i)),
                   pl.BlockSpec((1, PEER_GROUPS, PEER_TOPK, tn), lambda b, i: (b, 0, 0, i))],
        out_shape=[jax.ShapeDtypeStruct((bsz, PEER_HEADS * N_STATS, s), F32),
                   jax.ShapeDtypeStruct((bsz, PEER_GROUPS, PEER_TOPK, s), F32)],
        scratch_shapes=[pltpu.VMEM((2, PEER_CAND_ROWS, V7X_LANES), F32)],
        compiler_params=_cparams(("parallel", "parallel")),
        name="peer_select",
    )(st)


def _pick_row(rows, decisions, weights, index):
    if not decisions:
        return rows[index:index + 1, :]
    return jnp.where(decisions[0], _pick_row(rows, decisions[1:], weights[1:], index + weights[0]),
                     _pick_row(rows, decisions[1:], weights[1:], index))


def _prefix_count(rows, test):
    weights = [PEER_TOPK >> (k + 1) for k in range(PEER_TOPK.bit_length() - 1)]
    decisions = []
    for step in weights:
        decisions.append(test(_pick_row(rows, decisions, weights, step - 1)))
    count = jnp.where(decisions[0], float(weights[0]), 0.0)
    for d, w in zip(decisions[1:], weights[1:]):
        count = count + jnp.where(d, float(w), 0.0)
    return jnp.where(test(rows[PEER_TOPK - 1:PEER_TOPK, :]), float(PEER_TOPK), count)


def _peer_experts_kernel(ht_ref, st_ref, stats_ref, tops_ref, u_ref, vt_ref, o_ref,
                         len1_ref, e1_ref, rank2_ref, e2_ref, pre_ref, coef_ref, *, te, n_chunks):
    j = pl.program_id(2)
    tm = ht_ref.shape[-1]
    lanes = V7X_LANES
    rows_per_step = te // N_KEYS

    @pl.when(j == 0)
    def _():
        o_ref[...] = jnp.zeros(o_ref.shape, F32)
        coef_ref[1] = jnp.zeros(coef_ref.shape[1:], BF16)
        for h in range(PEER_HEADS):
            st = lambda k: stats_ref[0, h * N_STATS + k:h * N_STATS + k + 1, :]
            s1 = st_ref[0, (2 * h) * N_KEYS:(2 * h + 1) * N_KEYS, :]
            s2 = st_ref[0, (2 * h + 1) * N_KEYS:(2 * h + 2) * N_KEYS, :]
            v2 = tops_ref[0, 2 * h + 1]
            in1 = s1 >= st(0)
            in2 = s2 >= st(1)
            a1 = jnp.where(in1, s1, NEG)
            len1 = _prefix_count(v2, lambda val: a1 + val >= st(2))
            rank2 = _prefix_count(v2, lambda val: val > s2)
            len1_ref[h] = len1
            e1_ref[h] = jnp.where(in1, jnp.exp(s1 - st(3)) * st(5), 0.0)
            rank2_ref[h] = jnp.where(in2, rank2, float(PEER_TOPK)).astype(GATE_DTYPE)
            e2_ref[h] = jnp.where(in2, jnp.exp(s2 - st(4)), 0.0).astype(GATE_DTYPE)

    def add_previous_chunk():
        o_ref[0] += jnp.dot(vt_ref[...], coef_ref[(j + 1) % 2], preferred_element_type=F32)

    @pl.when(j < n_chunks)
    def _():
        ht = ht_ref[0]
        for r in range(rows_per_step):
            rs = slice(r * N_KEYS, (r + 1) * N_KEYS)
            pre_ref[rs, :] = jnp.dot(u_ref[rs, :], ht, preferred_element_type=F32)
        add_previous_chunk()
        first = pl.multiple_of(j * rows_per_step, rows_per_step)
        slot = j % 2
        zero = jnp.zeros((N_KEYS, GATE_LANES), GATE_DTYPE)
        for c in range(tm // GATE_LANES):
            cs = slice(c * GATE_LANES, (c + 1) * GATE_LANES)
            for r in range(rows_per_step):
                rs = slice(r * N_KEYS, (r + 1) * N_KEYS)
                gate = zero
                for h in range(PEER_HEADS):
                    row = lambda ref: jnp.broadcast_to(ref[h, pl.ds(first, rows_per_step), cs][r:r + 1, :],
                                                       (N_KEYS, GATE_LANES)).astype(GATE_DTYPE)
                    gate = gate + jnp.where(rank2_ref[h, :, cs] < row(len1_ref), row(e1_ref) * e2_ref[h, :, cs], zero)
                x = pre_ref[rs, cs]
                gelu = 0.5 * x * (1.0 + lax.erf(x * np.float32(math.sqrt(0.5))))
                coef_ref[slot, rs, cs] = (gate * gelu.astype(GATE_DTYPE)).astype(BF16)

    @pl.when(j == n_chunks)
    def _():
        add_previous_chunk()


def _peer_experts(hn2t, st, stats, tops, peer_u, peer_v, tm=512, te=2048):
    bsz, d, s = hn2t.shape
    n_exp = peer_u.shape[0]
    n_chunks = n_exp // te
    u = peer_u.astype(BF16)
    vt = peer_v.T.astype(BF16)
    tok = lambda n: pl.BlockSpec((1, n, tm), lambda b, i, j: (b, 0, i))
    table = lambda dt: pltpu.VMEM((PEER_HEADS, N_KEYS, tm), dt)
    return pl.pallas_call(
        functools.partial(_peer_experts_kernel, te=te, n_chunks=n_chunks),
        grid=(bsz, s // tm, n_chunks + 1),
        in_specs=[tok(d), tok(st.shape[1]), tok(stats.shape[1]),
                  pl.BlockSpec((1, PEER_GROUPS, PEER_TOPK, tm), lambda b, i, j: (b, 0, 0, i)),
                  pl.BlockSpec((te, d), lambda b, i, j: (jnp.minimum(j, n_chunks - 1), 0)),
                  pl.BlockSpec((d, te), lambda b, i, j: (0, jnp.maximum(j - 1, 0)))],
        out_specs=tok(d),
        out_shape=jax.ShapeDtypeStruct((bsz, d, s), F32),
        scratch_shapes=[table(F32), table(F32), table(GATE_DTYPE), table(GATE_DTYPE),
                        pltpu.VMEM((te, tm), F32), pltpu.VMEM((2, te, tm), BF16)],
        compiler_params=_cparams(("parallel", "parallel", "arbitrary")),
        name="peer_experts",
    )(hn2t, st, stats, tops, u, vt)


def _final_kernel(x1_ref, ft_ref, mod_ref, g_ref, o_ref):
    x2 = x1_ref[0] + mod_ref[0, 5:6, :] * ft_ref[0].T
    y = x2 * lax.rsqrt(jnp.mean(x2 * x2, axis=-1, keepdims=True) + EPS)
    o_ref[0] = y * g_ref[...]


def _final(x1, ft, mod, final_g, tm=512):
    bsz, s, d = x1.shape
    return pl.pallas_call(
        _final_kernel,
        grid=(bsz, s // tm),
        in_specs=[pl.BlockSpec((1, tm, d), lambda b, i: (b, i, 0)),
                  pl.BlockSpec((1, d, tm), lambda b, i: (b, 0, i)),
                  pl.BlockSpec((1, 6, d), lambda b, i: (b, 0, 0)), _const_spec((1, d))],
        out_specs=pl.BlockSpec((1, tm, d), lambda b, i: (b, i, 0)),
        out_shape=jax.ShapeDtypeStruct((bsz, s, d), F32),
        compiler_params=_cparams(("parallel", "parallel")),
        name="final_norm",
    )(x1, ft, mod, final_g.reshape(1, d))


def kernel(x, c, w_ada, b_ada, norm1_g, norm2_g, w_in, rel_bias, w_proj_a, w_proj_b, w_out,
           peer_wq, peer_sub_keys, peer_u, peer_v, final_g):
    bsz, s, d = x.shape
    mod = _ada(c, w_ada[0], b_ada[0]).reshape(bsz, 6, d)
    ka, ki, qb, vb, ga, gb, qat, vat, qit, wit, kbt = _in_proj(x, mod, norm1_g[0], w_in[0])
    ya = _dsa_t(qit, wit, ki, qat, ka, vat, _near_bias(rel_bias, DSA_T_TQ), DSA_T_TQ)
    yb = _stick_breaking(qb, kbt, vb)
    x1, hn2t, st = _mix(x, ya, yb, ga, gb, mod, norm2_g[0], w_proj_a[0], w_proj_b[0], w_out[0],
                        peer_wq[0], peer_sub_keys[0])
    stats, tops = _peer_select(st)
    ft = _peer_experts(hn2t, st, stats, tops, peer_u[0], peer_v[0])
    return _final(x1, ft, mod, final_g)
```

```python
import functools
import math

import jax
import jax.numpy as jnp
import numpy as np
from jax import lax
from jax.experimental import pallas as pl
from jax.experimental.pallas import tpu as pltpu

N_HEADS = 8
HEAD_DIM = 64
WIDTH = N_HEADS * HEAD_DIM
IDX_DIM = 64
TOPK_MAX = 256
N_BUCKETS = 32
MAX_DISTANCE = 128
EPS = 1e-6

V7X_LANES = 128
V7X_VMEM_LIMIT = 56 * 1024 * 1024

F32 = jnp.float32
BF16 = jnp.bfloat16


def _cparams(sem):
    return pltpu.CompilerParams(dimension_semantics=sem, vmem_limit_bytes=V7X_VMEM_LIMIT)


def _const_spec(shape):
    return pl.BlockSpec(shape, lambda *_: (0,) * len(shape), pipeline_mode=pl.Buffered(1))


def _ada_kernel(c_ref, w_ref, b_ref, o_ref):
    c = c_ref[...]
    cond = c * jax.nn.sigmoid(c)
    o_ref[...] = jnp.dot(cond, w_ref[...], preferred_element_type=F32,
                         precision=lax.Precision.HIGHEST) + b_ref[...]


def _ada(c, w, b):
    bsz, d = c.shape
    n = w.shape[1]
    tn = 1536
    return pl.pallas_call(
        _ada_kernel,
        grid=(n // tn,),
        in_specs=[pl.BlockSpec((bsz, d), lambda j: (0, 0)),
                  pl.BlockSpec((d, tn), lambda j: (0, j)),
                  pl.BlockSpec((1, tn), lambda j: (0, j))],
        out_specs=pl.BlockSpec((bsz, tn), lambda j: (0, j)),
        out_shape=jax.ShapeDtypeStruct((bsz, n), F32),
        compiler_params=_cparams(("arbitrary",)),
        name="ada",
    )(c, w, b.reshape(1, n))


def _modulated_norm(x, g, shift, scale):
    y = x * lax.rsqrt(jnp.mean(x * x, axis=-1, keepdims=True) + EPS)
    return (y * g) * (1.0 + scale) + shift


WI_ROWS = 16


def _in_proj_kernel(x_ref, mod_ref, g_ref, wr_ref, wt_ref,
                    ka_ref, ki_ref, qb_ref, vb_ref, ga_ref, gb_ref,
                    qat_ref, vat_ref, qit_ref, wit_ref, kbt_ref):
    h = _modulated_norm(x_ref[0], g_ref[...], mod_ref[0, 0:1, :], mod_ref[0, 1:2, :]).astype(BF16)
    c0 = 0
    for ref in (ka_ref, ki_ref, qb_ref, vb_ref, ga_ref, gb_ref):
        n = ref.shape[-1]
        for part in range(0, n, WIDTH):
            pw = min(WIDTH, n - part)
            ref[0, :, part:part + pw] = jnp.dot(h, wr_ref[:, c0 + part:c0 + part + pw],
                                               preferred_element_type=F32).astype(ref.dtype)
        c0 += n
    nt = (((1,), (1,)), ((), ()))
    r0 = 0
    for ref in (qat_ref, vat_ref, qit_ref, wit_ref, kbt_ref):
        n = ref.shape[1]
        ref[0] = lax.dot_general(wt_ref[r0:r0 + n, :], h, nt, preferred_element_type=F32).astype(ref.dtype)
        r0 += n


def _in_proj(x, mod, g1, w_in, tm=512):
    bsz, s, d = x.shape
    w = WIDTH
    offs = np.cumsum([0, w, w, w, w, IDX_DIM, N_HEADS, w, w, w, d, d])
    col = lambda i: w_in[:, offs[i]:offs[i + 1]]
    qscale = HEAD_DIM ** -0.5
    w_rows = jnp.concatenate([col(1), col(4), col(4), col(6) * qscale, col(8), col(9), col(10)],
                             axis=1).astype(BF16)
    w_t = jnp.concatenate([col(0) * qscale, col(2), col(3),
                           jnp.pad(col(5), ((0, 0), (0, WI_ROWS - N_HEADS))), col(7)], axis=1).T.astype(BF16)

    row = lambda n, dt: jax.ShapeDtypeStruct((bsz, s, n), dt)
    tr = lambda n, dt: jax.ShapeDtypeStruct((bsz, n, s), dt)
    rspec = lambda n: pl.BlockSpec((1, tm, n), lambda b, i: (b, i, 0))
    tspec = lambda n: pl.BlockSpec((1, n, tm), lambda b, i: (b, 0, i))
    return pl.pallas_call(
        _in_proj_kernel,
        grid=(bsz, s // tm),
        in_specs=[rspec(d),
                  pl.BlockSpec((1, 6, d), lambda b, i: (b, 0, 0)),
                  _const_spec((1, d)),
                  _const_spec(w_rows.shape), _const_spec(w_t.shape)],
        out_specs=[rspec(w), rspec(2 * IDX_DIM), rspec(w), rspec(w), rspec(d), rspec(d),
                   tspec(w), tspec(w), tspec(w), tspec(WI_ROWS), tspec(w)],
        out_shape=[row(w, BF16), row(2 * IDX_DIM, BF16), row(w, BF16), row(w, BF16), row(d, F32), row(d, F32),
                   tr(w, BF16), tr(w, BF16), tr(w, BF16), tr(WI_ROWS, F32), tr(w, BF16)],
        compiler_params=_cparams(("parallel", "parallel")),
        name="in_proj",
    )(x, mod, g1.reshape(1, d), w_rows, w_t)


def _t5_bucket_np(n):
    max_exact = N_BUCKETS // 2
    nf = np.maximum(n, 1).astype(np.float32)
    large = max_exact + (np.log(nf / np.float32(max_exact)) / np.float32(math.log(MAX_DISTANCE / max_exact))
                         * np.float32(N_BUCKETS - max_exact)).astype(np.int32)
    large = np.minimum(large, N_BUCKETS - 1)
    return np.where(n < max_exact, n, large).astype(np.int32)


def _near_bias_kernel(bk_ref, rb_ref, o_ref):
    bk = bk_ref[...]
    for h in range(N_HEADS):
        far = rb_ref[N_BUCKETS - 1, h]
        acc = jnp.zeros(bk.shape, F32)
        for b in range(N_BUCKETS - 1):
            acc = jnp.where(bk == b, rb_ref[b, h] - far, acc)
        o_ref[h] = acc


def _near_bias(rel_bias, tq):
    a = np.arange(2 * tq)[:, None]
    b = np.arange(tq)[None, :]
    n = tq + b - a
    far_from = int(np.argmax(_t5_bucket_np(np.arange(4 * MAX_DISTANCE)) == N_BUCKETS - 1))
    assert far_from <= tq, "distances beyond the near zone must share the last bucket"
    bucket = np.where(n >= 0, _t5_bucket_np(np.maximum(n, 0)), N_BUCKETS - 1).astype(np.int32)
    return pl.pallas_call(
        _near_bias_kernel,
        in_specs=[pl.BlockSpec(memory_space=pltpu.VMEM), pl.BlockSpec(memory_space=pltpu.SMEM)],
        out_specs=pl.BlockSpec(memory_space=pltpu.VMEM),
        out_shape=jax.ShapeDtypeStruct((N_HEADS, 2 * tq, tq), F32),
        name="near_bias",
    )(jnp.asarray(bucket), rel_bias)


INT_MIN = -2 ** 31
NEG = -1e30


def _sortable(x):
    bits = pltpu.bitcast(x, jnp.int32)
    return bits ^ ((bits >> 31) & 0x7FFFFFFF)


DSA_T_TQ = 256
SUBLANES = 8
KEY_BITS = 32
IDX_SCALE = (IDX_DIM ** -0.5) * (N_HEADS ** -0.5)
assert DSA_T_TQ == KEY_BITS * SUBLANES, "one key block must fill the 32 bit positions of a sublane tile"


def _bit_transpose(rows):
    a = list(rows)
    j, m = KEY_BITS // 2, 0x0000FFFF
    while j:
        for k in range(KEY_BITS):
            if k & j:
                continue
            t = (a[k] ^ lax.shift_right_logical(a[k + j], jnp.int32(j))) & m
            a[k] = a[k] ^ t
            a[k + j] = a[k + j] ^ lax.shift_left(t, jnp.int32(j))
        j >>= 1
        m ^= m << j
    return a


def _dsa_t_kernel(qit_ref, wit_ref, ki_ref, qat_ref, ka_ref, vat_ref, nb_ref, o_ref,
                  sc_ref, planes_ref, alive_ref, thr_ref, cnt_ref, qim_ref, qam_ref, acc_ref, m_ref, l_ref,
                  lg_ref, bmax_ref,
                  *, tq, topk):
    i = pl.program_id(1)
    t0 = i * tq
    lanes = V7X_LANES
    sk = tq
    hd = HEAD_DIM
    idx_bits = (sc_ref.shape[0] - 1).bit_length()

    zeros = jnp.zeros((hd, tq), BF16)
    for h in range(N_HEADS):
        own = slice((h % 2) * hd, (h % 2 + 1) * hd)
        other = slice((1 - h % 2) * hd, (2 - h % 2) * hd)
        qim_ref[own, h * tq:(h + 1) * tq] = qit_ref[0, h * hd:(h + 1) * hd, :]
        qim_ref[other, h * tq:(h + 1) * tq] = zeros
        qam_ref[h, own, :] = qat_ref[0, h * hd:(h + 1) * hd, :]
        qam_ref[h, other, :] = zeros

    n_blocks = i + 1
    key_i = lax.broadcasted_iota(jnp.int32, (sk, tq), 0)
    qry_i = t0 + lax.broadcasted_iota(jnp.int32, (sk, tq), 1)
    w = wit_ref[0]

    def score_body(kb, carry):
        s0 = pl.multiple_of(kb * sk, sk)
        z = jnp.dot(ki_ref[0, pl.ds(s0, sk), :], qim_ref[...], preferred_element_type=F32)
        sc = w[0:1, :] * jnp.maximum(z[:, 0:tq], 0.0)
        for h in range(1, N_HEADS):
            sc = sc + w[h:h + 1, :] * jnp.maximum(z[:, h * tq:(h + 1) * tq], 0.0)
        sc = jnp.where(key_i + s0 <= qry_i, sc * IDX_SCALE, -jnp.inf)
        sc_ref[pl.ds(s0, sk), :] = sc
        ordered = _sortable(sc) ^ INT_MIN
        planes = _bit_transpose([ordered[g * SUBLANES:(g + 1) * SUBLANES, :] for g in range(KEY_BITS)])
        for p in range(KEY_BITS):
            planes_ref[p, kb] = planes[p]
        return carry

    lax.fori_loop(0, n_blocks, score_body, 0)

    def first_count(kb, cnt):
        alive_ref[kb] = jnp.full((SUBLANES, tq), -1, jnp.int32)
        return cnt + lax.population_count(planes_ref[0, kb])

    zero_cnt = jnp.zeros((SUBLANES, tq), jnp.int32)
    cnt0 = lax.fori_loop(0, n_blocks, first_count, zero_cnt)

    def bit_body(p, carry):
        r, k_left, cnt = carry
        total = jnp.sum(cnt, axis=0, keepdims=True)
        take = total >= k_left
        r = jnp.where(take, r | lax.shift_left(jnp.int32(1), KEY_BITS - 1 - p), r)
        k_left = jnp.where(take, k_left, k_left - total)
        drop = jnp.where(take, 0, -1)
        p_next = jnp.minimum(p + 1, KEY_BITS - 1)

        def narrow(kb, c):
            alive = alive_ref[kb] & (planes_ref[p, kb] ^ drop)
            alive_ref[kb] = alive
            return c + lax.population_count(alive & planes_ref[p_next, kb])

        return r, k_left, lax.fori_loop(0, n_blocks, narrow, zero_cnt)

    r, k_left, _ = lax.fori_loop(0, KEY_BITS, bit_body,
                                 (jnp.zeros((1, tq), jnp.int32), jnp.full((1, tq), topk, jnp.int32), cnt0))
    guess_bits = r ^ INT_MIN
    guess = pltpu.bitcast(guess_bits ^ ((guess_bits >> 31) & 0x7FFFFFFF), F32)

    def count_where(pred):
        def body(kb, c):
            hit = jnp.where(pred(sc_ref[pl.ds(pl.multiple_of(kb * sk, sk), sk), :]), 1, 0)
            for g in range(sk // SUBLANES):
                c = c + hit[g * SUBLANES:(g + 1) * SUBLANES, :]
            return c
        return jnp.sum(lax.fori_loop(0, n_blocks, body, zero_cnt), axis=0, keepdims=True)

    few = qry_i[0:1, :] + 1 < topk
    lowest = float(jnp.finfo(F32).min)
    n_ge = count_where(lambda s: s >= guess)
    n_gt = count_where(lambda s: s > guess)
    exact =jnp.logical_or(few, jnp.logical_and(n_ge >= topk, n_gt < topk))
    thr_ref[0:1, :] = jnp.where(few, lowest, guess)
    cnt_ref[0:1, :] = n_ge
    cnt_ref[1:2, :] = n_gt

    @pl.when(jnp.min(jnp.where(exact, 1, 0)) == 0)
    def _():
        def extreme(kb, c):
            blk = sc_ref[pl.ds(pl.multiple_of(kb * sk, sk), sk), :]
            return (jnp.minimum(c[0], jnp.min(jnp.where(blk > -jnp.inf, blk, jnp.inf), axis=0, keepdims=True)),
                    jnp.maximum(c[1], jnp.max(blk, axis=0, keepdims=True)))

        lo, hi = lax.fori_loop(0, n_blocks, extreme,
                               (jnp.full((1, tq), jnp.inf, F32), jnp.full((1, tq), -jnp.inf, F32)))
        top = hi

        def unsettled(state):
            lo, hi = state
            mid = 0.5 * lo + 0.5 * hi
            open_ = jnp.logical_and(jnp.logical_not(exact), jnp.logical_and(mid > lo, mid < hi))
            return jnp.max(jnp.where(open_, 1, 0)) > 0

        def halve(state):
            lo, hi = state
            mid = 0.5 * lo + 0.5 * hi
            up = count_where(lambda s: s >= mid) >= topk
            return jnp.where(up, mid, lo), jnp.where(up, hi, mid)

        lo, hi = lax.while_loop(unsettled, halve, (lo, hi))
        found = jnp.where(count_where(lambda s: s >= top) >= topk, top, lo)
        thr_new = jnp.where(exact, thr_ref[0:1, :], found)
        thr_ref[0:1, :] = thr_new
        cnt_ref[0:1, :] = count_where(lambda s: s >= thr_new)
        cnt_ref[1:2, :] = count_where(lambda s: s > thr_new)

    thr = thr_ref[0:1, :]

    keep_tied = topk - cnt_ref[1:2, :]
    surplus = jnp.logical_and(cnt_ref[0:1, :] > topk, jnp.logical_not(few))

    @pl.when(jnp.max(jnp.where(surplus, 1, 0)) > 0)
    def _():
        def tied_before(cand):
            def body(kb, c):
                s0 = pl.multiple_of(kb * sk, sk)
                hit = jnp.where(sc_ref[pl.ds(s0, sk), :] == thr, jnp.where(key_i + s0 < cand, 1, 0), 0)
                for g in range(sk // SUBLANES):
                    c = c + hit[g * SUBLANES:(g + 1) * SUBLANES, :]
                return c
            return jnp.sum(lax.fori_loop(0, n_blocks, body, zero_cnt), axis=0, keepdims=True)

        def index_bit(it, c):
            cand = c | lax.shift_left(jnp.int32(1), idx_bits - 1 - it)
            return jnp.where(tied_before(cand) < keep_tied, cand, c)

        last_kept = lax.fori_loop(0, idx_bits, index_bit, jnp.zeros((1, tq), jnp.int32))

        def remove(kb, carry):
            s0 = pl.multiple_of(kb * sk, sk)
            sc = sc_ref[pl.ds(s0, sk), :]
            drop = jnp.where(sc == thr, jnp.where(jnp.logical_and(key_i + s0 > last_kept, surplus), 1, 0), 0)
            sc_ref[pl.ds(s0, sk), :] = jnp.where(drop > 0, -jnp.inf, sc)
            return carry

        lax.fori_loop(0, n_blocks, remove, 0)

    m_ref[...] = jnp.full(m_ref.shape, NEG, F32)
    l_ref[...] = jnp.zeros(l_ref.shape, F32)
    acc_ref[...] = jnp.zeros(acc_ref.shape, F32)

    def logits(kb, slot, bias_row0):
        s0 = pl.multiple_of(kb * sk, sk)
        mask = jnp.where(sc_ref[pl.ds(s0, sk), :] >= thr, 0.0, NEG)
        for h in range(N_HEADS):
            p = h // 2
            lg = jnp.dot(ka_ref[0, pl.ds(s0, sk), p * lanes:(p + 1) * lanes], qam_ref[h],
                         preferred_element_type=F32) + mask
            if bias_row0 is not None:
                lg = lg + nb_ref[h, bias_row0:bias_row0 + sk, :]
            lg_ref[slot, h] = lg
            bmax_ref[slot, h, 0:1, :] = jnp.max(lg, axis=0, keepdims=True)

    ones_rows = jnp.ones((2 * SUBLANES, sk), BF16)

    def accumulate(kb, slot):
        s0 = pl.multiple_of(kb * sk, sk)
        for h in range(N_HEADS):
            hs = slice(h * hd, (h + 1) * hd)
            m_old = m_ref[h, 0:1, :]
            m_new = jnp.maximum(m_old, bmax_ref[slot, h, 0:1, :])
            alpha = jnp.exp(m_old - m_new)
            pr = jnp.exp(lg_ref[slot, h] - m_new).astype(BF16)
            values = jnp.concatenate([vat_ref[0, hs, pl.ds(s0, sk)], ones_rows], axis=0)
            both = jnp.dot(values, pr, preferred_element_type=F32)
            l_ref[h, 0:1, :] = alpha * l_ref[h, 0:1, :] + both[hd:hd + 1, :]
            acc_ref[hs, :] = alpha * acc_ref[hs, :] + both[0:hd, :]
            m_ref[h, 0:1, :] = m_new

    n_far = jnp.maximum(i - 1, 0)

    @pl.when(n_far > 0)
    def _():
        logits(0, 0, None)

    def far_body(kb, carry):
        logits(kb, kb % 2, None)
        accumulate(kb - 1, (kb - 1) % 2)
        return carry

    lax.fori_loop(1, n_far, far_body, 0)

    @pl.when(n_far > 0)
    def _():
        accumulate(n_far - 1, (n_far - 1) % 2)

    @pl.when(i > 0)
    def _():
        logits(i - 1, 0, 0)
        logits(i, 1, sk)
        accumulate(i - 1, 0)
        accumulate(i, 1)

    @pl.when(i == 0)
    def _():
        logits(i, 1, sk)
        accumulate(i, 1)

    for h in range(N_HEADS):
        hs = slice(h * hd, (h + 1) * hd)
        acc_ref[hs, :] = acc_ref[hs, :] / l_ref[h, 0:1, :]
    o_ref[0] = acc_ref[...].T.astype(o_ref.dtype)


def _dsa_t(qit, wit, ki, qat, ka, vat, near_bias, tq):
    bsz, w, s = qat.shape
    topk = min(TOPK_MAX, s // 4)
    tspec = lambda n: pl.BlockSpec((1, n, tq), lambda b, i: (b, 0, i))
    resident = lambda shape: pl.BlockSpec((1,) + shape, lambda b, i: (b, 0, 0), pipeline_mode=pl.Buffered(1))
    return pl.pallas_call(
        functools.partial(_dsa_t_kernel, tq=tq, topk=topk),
        grid=(bsz, s // tq),
        in_specs=[tspec(w), tspec(WI_ROWS), resident((s, 2 * IDX_DIM)),
                  tspec(w), resident((s, w)), resident((w, s)),
                  _const_spec(near_bias.shape)],
        out_specs=pl.BlockSpec((1, tq, w), lambda b, i: (b, i, 0)),
        out_shape=jax.ShapeDtypeStruct((bsz, s, w), BF16),
        scratch_shapes=[pltpu.VMEM((s, tq), F32),
                        pltpu.VMEM((KEY_BITS, s // tq, SUBLANES, tq), jnp.int32),
                        pltpu.VMEM((s // tq, SUBLANES, tq), jnp.int32),
                        pltpu.VMEM((SUBLANES, tq), F32),
                        pltpu.VMEM((SUBLANES, tq), jnp.int32),
                        pltpu.VMEM((2 * HEAD_DIM, N_HEADS * tq), BF16),
                        pltpu.VMEM((N_HEADS, 2 * HEAD_DIM, tq), BF16),
                        pltpu.VMEM((w, tq), F32),
                        pltpu.VMEM((N_HEADS, SUBLANES, tq), F32),
                        pltpu.VMEM((N_HEADS, SUBLANES, tq), F32),
                        pltpu.VMEM((2, N_HEADS, tq, tq), F32),
                        pltpu.VMEM((2, N_HEADS, SUBLANES, tq), F32)],
        compiler_params=_cparams(("arbitrary", "arbitrary")),
        name="dsa",
    )(qit, wit, ki, qat, ka, vat, near_bias)


SB_TQ = 256
SB_PAIRS = 2
SB_DEAD = -104.0


def _sb_kernel(q_ref, kt_ref, v_ref, o_ref, *, tq, pairs):
    i = pl.program_id(2)
    lanes = V7X_LANES
    heads = 2 * pairs
    lo_half = lax.broadcasted_iota(jnp.int32, (tq, lanes), 1) < HEAD_DIM
    r_i = lax.broadcasted_iota(jnp.int32, (tq, tq), 0)
    c_i = lax.broadcasted_iota(jnp.int32, (tq, tq), 1)
    strict = c_i < r_i
    after = jnp.where(r_i > c_i, 1.0, 0.0).astype(BF16)
    after2 = jnp.concatenate([after, after], axis=0)
    qms = []
    for p in range(pairs):
        q = q_ref[0, :, p * lanes:(p + 1) * lanes].astype(F32)
        qms += [jnp.where(lo_half, q, 0.0).astype(BF16), jnp.where(lo_half, 0.0, q).astype(BF16)]

    def tile(kb, diag, state):
        s0 = pl.multiple_of(kb * tq, tq)
        carries, accs = [], []
        for hh in range(heads):
            p = hh // 2
            carry, acc = state[hh], state[heads + hh]
            z = jnp.dot(qms[hh], kt_ref[0, p * lanes:(p + 1) * lanes, pl.ds(s0, tq)], preferred_element_type=F32)
            sp = jnp.log(1.0 + jnp.exp(-jnp.abs(z)))
            log_beta = jnp.minimum(z, 0.0) - sp
            log_rest = -(jnp.maximum(z, 0.0) + sp)
            if diag:
                log_rest = jnp.where(strict, log_rest, 0.0)
            hi = log_rest.astype(BF16)
            lo = (log_rest - hi.astype(F32)).astype(BF16)
            suffix = jnp.dot(jnp.concatenate([hi, lo], axis=1), after2, preferred_element_type=F32)
            a = jnp.exp(log_beta + suffix + carry)
            if diag:
                a = jnp.where(strict, a, 0.0)
            accs.append(acc + jnp.dot(a.astype(BF16), v_ref[0, pl.ds(s0, tq), p * lanes:(p + 1) * lanes],
                                      preferred_element_type=F32))
            carries.append(carry + jnp.sum(log_rest, axis=1, keepdims=True))
        return tuple(carries) + tuple(accs)

    zc = jnp.zeros((tq, 1), F32)
    za = jnp.zeros((tq, lanes), F32)
    state = tile(i, True, (zc,) * heads + (za,) * heads)

    def alive(ls):
        kb, state = ls
        worst = state[0]
        for hh in range(1, heads):
            worst = jnp.maximum(worst, state[hh])
        return jnp.logical_and(kb >= 0, jnp.max(worst) >= SB_DEAD)

    def body(ls):
        kb, state = ls
        return kb - 1, tile(kb, False, state)

    _, state = lax.while_loop(alive, body, (i - 1, state))
    for p in range(pairs):
        o_ref[0, :, p * lanes:(p + 1) * lanes] = jnp.where(
            lo_half, state[heads + 2 * p], state[heads + 2 * p + 1]).astype(o_ref.dtype)


def _stick_breaking(qb, kbt, vb, tq=SB_TQ, pairs=SB_PAIRS):
    bsz, s, w = qb.shape
    lanes = V7X_LANES * pairs
    return pl.pallas_call(
        functools.partial(_sb_kernel, tq=tq, pairs=pairs),
        grid=(bsz, w // lanes, s // tq),
        in_specs=[pl.BlockSpec((1, tq, lanes), lambda b, p, i: (b, i, p)),
                  pl.BlockSpec((1, lanes, s), lambda b, p, i: (b, p, 0)),
                  pl.BlockSpec((1, s, lanes), lambda b, p, i: (b, 0, p))],
        out_specs=pl.BlockSpec((1, tq, lanes), lambda b, p, i: (b, i, p)),
        out_shape=jax.ShapeDtypeStruct((bsz, s, w), BF16),
        compiler_params=_cparams(("arbitrary", "arbitrary", "arbitrary")),
        name="stick_breaking",
    )(qb, kbt, vb)


PEER_HEADS = 8
N_KEYS = 128
PEER_HALF = 128
PEER_TOPK = 16
PEER_GROUPS = 2 * PEER_HEADS


def _mix_kernel(x_ref, ya_ref, yb_ref, ga_ref, gb_ref, mod_ref, g2_ref, wa_ref, wb_ref, wo_ref, wq_ref,
                sk_ref, x1_ref, hn2t_ref, st_ref):
    ma = jnp.dot(ya_ref[0], wa_ref[...], preferred_element_type=F32)
    mb = jnp.dot(yb_ref[0], wb_ref[...], preferred_element_type=F32)
    merged = jax.nn.sigmoid(ga_ref[0]) * ma + jax.nn.sigmoid(gb_ref[0]) * mb
    x1 = x_ref[0] + mod_ref[0, 2:3, :] * jnp.dot(merged.astype(BF16), wo_ref[...], preferred_element_type=F32)
    x1_ref[0] = x1
    h2 = _modulated_norm(x1, g2_ref[...], mod_ref[0, 3:4, :], mod_ref[0, 4:5, :])
    hn2t_ref[0] = h2.T.astype(BF16)
    qh = jnp.dot(h2.astype(BF16), wq_ref[...], preferred_element_type=F32).astype(BF16)
    nt = (((1,), (1,)), ((), ()))
    for g in range(PEER_GROUPS):
        st_ref[0, g * N_KEYS:(g + 1) * N_KEYS, :] = lax.dot_general(
            sk_ref[g], qh[:, g * PEER_HALF:(g + 1) * PEER_HALF], nt, preferred_element_type=F32)


def _mix(x, ya, yb, ga, gb, mod, g2, w_proj_a, w_proj_b, w_out, peer_wq, sub_keys, tm=512):
    bsz, s, d = x.shape
    w = ya.shape[-1]
    nq = peer_wq.shape[1]
    sk = sub_keys.reshape(PEER_GROUPS, N_KEYS, PEER_HALF).astype(BF16)
    rspec = lambda n: pl.BlockSpec((1, tm, n), lambda b, i: (b, i, 0))
    tspec = lambda n: pl.BlockSpec((1, n, tm), lambda b, i: (b, 0, i))
    return pl.pallas_call(
        _mix_kernel,
        grid=(bsz, s // tm),
        in_specs=[rspec(d), rspec(w), rspec(w), rspec(d), rspec(d),
                  pl.BlockSpec((1, 6, d), lambda b, i: (b, 0, 0)), _const_spec((1, d)),
                  _const_spec((w, d)), _const_spec((w, d)), _const_spec((d, d)), _const_spec((d, nq)),
                  _const_spec(sk.shape)],
        out_specs=[rspec(d), tspec(d), tspec(PEER_GROUPS * N_KEYS)],
        out_shape=[jax.ShapeDtypeStruct((bsz, s, d), F32),
                   jax.ShapeDtypeStruct((bsz, d, s), BF16),
                   jax.ShapeDtypeStruct((bsz, PEER_GROUPS * N_KEYS, s), F32)],
        compiler_params=_cparams(("parallel", "parallel")),
        name="mix_out",
    )(x, ya, yb, ga, gb, mod, g2.reshape(1, d), w_proj_a.astype(BF16), w_proj_b.astype(BF16),
      w_out.astype(BF16), peer_wq.astype(BF16), sk)


N_STATS = 8
SELECT_WAYS = 4
GATE_DTYPE = BF16
GATE_LANES = 512
PEER_CAND_ROWS = -(-sum(PEER_TOPK // (k + 1) for k in range(PEER_TOPK)) // 8) * 8


def _peer_select_kernel(st_ref, stats_ref, tops_ref, cand_ref):
    tn = st_ref.shape[-1]
    lanes = V7X_LANES
    top_ref = tops_ref.at[0]

    def chunk_body(c, carry):
        cs = pl.ds(pl.multiple_of(c * lanes, lanes), lanes)

        def groups_body(gq, carry):
            gs = [gq * SELECT_WAYS + w for w in range(SELECT_WAYS)]
            sc = [st_ref[0, pl.ds(pl.multiple_of(g * N_KEYS, N_KEYS), N_KEYS), cs] for g in gs]
            for k in range(PEER_TOPK):
                for w, g in enumerate(gs):
                    m = jnp.max(sc[w], axis=0, keepdims=True)
                    top_ref[g, k:k + 1, cs] = m
                    sc[w] = jnp.where(sc[w] == m, -jnp.inf, sc[w])
            return carry

        lax.fori_loop(0, PEER_GROUPS // SELECT_WAYS, groups_body, 0)

        def heads_body(hq, carry):
            hs = [hq * 2, hq * 2 + 1]
            v1 = [top_ref[2 * h, :, cs] for h in hs]
            v2 = [top_ref[2 * h + 1, :, cs] for h in hs]
            cand = []
            for w in range(2):
                cand_ref[w] = jnp.full(cand_ref.shape[1:], -jnp.inf, F32)
                row = 0
                for k in range(PEER_TOPK):
                    n = PEER_TOPK // (k + 1)
                    cand_ref[w, row:row + n, :] = v1[w][k:k + 1, :] + v2[w][0:n, :]
                    row += n
                cand.append(cand_ref[w])
            sc = list(cand)
            for k in range(PEER_TOPK):
                mk = [jnp.max(sc[w], axis=0, keepdims=True) for w in range(2)]
                sc = [jnp.where(sc[w] == mk[w], -jnp.inf, sc[w]) for w in range(2)]
            for w, h in enumerate(hs):
                thr = mk[w]
                m = v1[w][0:1, :] + v2[w][0:1, :]
                z = jnp.sum(jnp.where(cand[w] >= thr, jnp.exp(cand[w] - m), 0.0), axis=0, keepdims=True)
                rows = [v1[w][PEER_TOPK - 1:PEER_TOPK, :], v2[w][PEER_TOPK - 1:PEER_TOPK, :], thr,
                        v1[w][0:1, :], v2[w][0:1, :], 1.0 / z, jnp.zeros((2, lanes), F32)]
                stats_ref[0, pl.ds(pl.multiple_of(h * N_STATS, N_STATS), N_STATS), cs] = jnp.concatenate(rows, axis=0)
            return carry

        lax.fori_loop(0, PEER_HEADS // 2, heads_body, 0)
        return carry

    lax.fori_loop(0, tn // lanes, chunk_body, 0)


def _peer_select(st, tn=512):
    bsz, rows, s = st.shape
    return pl.pallas_call(
        _peer_select_kernel,
        grid=(bsz, s // tn),
        in_specs=[pl.BlockSpec((1, rows, tn), lambda b, i: (b, 0, i))],
        out_specs=[pl.BlockSpec((1, PEER_HEADS * N_STATS, tn), lambda b, i: (b, 0, i)),
                   pl.BlockSpec((1, PEER_GROUPS, PEER_TOPK, tn), lambda b, i: (b, 0, 0, i))],
        out_shape=[jax.ShapeDtypeStruct((bsz, PEER_HEADS * N_STATS, s), F32),
                   jax.ShapeDtypeStruct((bsz, PEER_GROUPS, PEER_TOPK, s), F32)],
        scratch_shapes=[pltpu.VMEM((2, PEER_CAND_ROWS, V7X_LANES), F32)],
        compiler_params=_cparams(("parallel", "parallel")),
        name="peer_select",
    )(st)


def _pick_row(rows, decisions, weights, index):
    if not decisions:
        return rows[index:index + 1, :]
    return jnp.where(decisions[0], _pick_row(rows, decisions[1:], weights[1:], index + weights[0]),
                     _pick_row(rows, decisions[1:], weights[1:], index))


def _prefix_count(rows, test):
    weights = [PEER_TOPK >> (k + 1) for k in range(PEER_TOPK.bit_length() - 1)]
    decisions = []
    for step in weights:
        decisions.append(test(_pick_row(rows, decisions, weights, step - 1)))
    count = jnp.where(decisions[0], float(weights[0]), 0.0)
    for d, w in zip(decisions[1:], weights[1:]):
        count = count + jnp.where(d, float(w), 0.0)
    return jnp.where(test(rows[PEER_TOPK - 1:PEER_TOPK, :]), float(PEER_TOPK), count)


def _peer_experts_kernel(ht_ref, st_ref, stats_ref, tops_ref, u_ref, vt_ref, o_ref,
                         len1_ref, e1_ref, rank2_ref, e2_ref, pre_ref, coef_ref, *, te, n_chunks):
    j = pl.program_id(2)
    tm = ht_ref.shape[-1]
    lanes = V7X_LANES
    rows_per_step = te // N_KEYS

    @pl.when(j == 0)
    def _():
        o_ref[...] = jnp.zeros(o_ref.shape, F32)
        coef_ref[1] = jnp.zeros(coef_ref.shape[1:], BF16)
        for h in range(PEER_HEADS):
            st = lambda k: stats_ref[0, h * N_STATS + k:h * N_STATS + k + 1, :]
            s1 = st_ref[0, (2 * h) * N_KEYS:(2 * h + 1) * N_KEYS, :]
            s2 = st_ref[0, (2 * h + 1) * N_KEYS:(2 * h + 2) * N_KEYS, :]
            v2 = tops_ref[0, 2 * h + 1]
            in1 = s1 >= st(0)
            in2 = s2 >= st(1)
            a1 = jnp.where(in1, s1, NEG)
            len1 = _prefix_count(v2, lambda val: a1 + val >= st(2))
            rank2 = _prefix_count(v2, lambda val: val > s2)
            len1_ref[h] = len1
            e1_ref[h] = jnp.where(in1, jnp.exp(s1 - st(3)) * st(5), 0.0)
            rank2_ref[h] = jnp.where(in2, rank2, float(PEER_TOPK)).astype(GATE_DTYPE)
            e2_ref[h] = jnp.where(in2, jnp.exp(s2 - st(4)), 0.0).astype(GATE_DTYPE)

    def add_previous_chunk():
        o_ref[0] += jnp.dot(vt_ref[...], coef_ref[(j + 1) % 2], preferred_element_type=F32)

    @pl.when(j < n_chunks)
    def _():
        ht = ht_ref[0]
        for r in range(rows_per_step):
            rs = slice(r * N_KEYS, (r + 1) * N_KEYS)
            pre_ref[rs, :] = jnp.dot(u_ref[rs, :], ht, preferred_element_type=F32)
        add_previous_chunk()
        first = pl.multiple_of(j * rows_per_step, rows_per_step)
        slot = j % 2
        zero = jnp.zeros((N_KEYS, GATE_LANES), GATE_DTYPE)
        for c in range(tm // GATE_LANES):
            cs = slice(c * GATE_LANES, (c + 1) * GATE_LANES)
            for r in range(rows_per_step):
                rs = slice(r * N_KEYS, (r + 1) * N_KEYS)
                gate = zero
                for h in range(PEER_HEADS):
                    row = lambda ref: jnp.broadcast_to(ref[h, pl.ds(first, rows_per_step), cs][r:r + 1, :],
                                                       (N_KEYS, GATE_LANES)).astype(GATE_DTYPE)
                    gate = gate + jnp.where(rank2_ref[h, :, cs] < row(len1_ref), row(e1_ref) * e2_ref[h, :, cs], zero)
                x = pre_ref[rs, cs]
                gelu = 0.5 * x * (1.0 + lax.erf(x * np.float32(math.sqrt(0.5))))
                coef_ref[slot, rs, cs] = (gate * gelu.astype(GATE_DTYPE)).astype(BF16)

    @pl.when(j == n_chunks)
    def _():
        add_previous_chunk()


def _peer_experts(hn2t, st, stats, tops, peer_u, peer_v, tm=512, te=2048):
    bsz, d, s = hn2t.shape
    n_exp = peer_u.shape[0]
    n_chunks = n_exp // te
    u = peer_u.astype(BF16)
    vt = peer_v.T.astype(BF16)
    tok = lambda n: pl.BlockSpec((1, n, tm), lambda b, i, j: (b, 0, i))
    table = lambda dt: pltpu.VMEM((PEER_HEADS, N_KEYS, tm), dt)
    return pl.pallas_call(
        functools.partial(_peer_experts_kernel, te=te, n_chunks=n_chunks),
        grid=(bsz, s // tm, n_chunks + 1),
        in_specs=[tok(d), tok(st.shape[1]), tok(stats.shape[1]),
                  pl.BlockSpec((1, PEER_GROUPS, PEER_TOPK, tm), lambda b, i, j: (b, 0, 0, i)),
                  pl.BlockSpec((te, d), lambda b, i, j: (jnp.minimum(j, n_chunks - 1), 0)),
                  pl.BlockSpec((d, te), lambda b, i, j: (0, jnp.maximum(j - 1, 0)))],
        out_specs=tok(d),
        out_shape=jax.ShapeDtypeStruct((bsz, d, s), F32),
        scratch_shapes=[table(F32), table(F32), table(GATE_DTYPE), table(GATE_DTYPE),
                        pltpu.VMEM((te, tm), F32), pltpu.VMEM((2, te, tm), BF16)],
        compiler_params=_cparams(("parallel", "parallel", "arbitrary")),
        name="peer_experts",
    )(hn2t, st, stats, tops, u, vt)


def _final_kernel(x1_ref, ft_ref, mod_ref, g_ref, o_ref):
    x2 = x1_ref[0] + mod_ref[0, 5:6, :] * ft_ref[0].T
    y = x2 * lax.rsqrt(jnp.mean(x2 * x2, axis=-1, keepdims=True) + EPS)
    o_ref[0] = y * g_ref[...]


def _final(x1, ft, mod, final_g, tm=512):
    bsz, s, d = x1.shape
    return pl.pallas_call(
        _final_kernel,
        grid=(bsz, s // tm),
        in_specs=[pl.BlockSpec((1, tm, d), lambda b, i: (b, i, 0)),
                  pl.BlockSpec((1, d, tm), lambda b, i: (b, 0, i)),
                  pl.BlockSpec((1, 6, d), lambda b, i: (b, 0, 0)), _const_spec((1, d))],
        out_specs=pl.BlockSpec((1, tm, d), lambda b, i: (b, i, 0)),
        out_shape=jax.ShapeDtypeStruct((bsz, s, d), F32),
        compiler_params=_cparams(("parallel", "parallel")),
        name="final_norm",
    )(x1, ft, mod, final_g.reshape(1, d))


def kernel(x, c, w_ada, b_ada, norm1_g, norm2_g, w_in, rel_bias, w_proj_a, w_proj_b, w_out,
           peer_wq, peer_sub_keys, peer_u, peer_v, final_g):
    bsz, s, d = x.shape
    mod = _ada(c, w_ada[0], b_ada[0]).reshape(bsz, 6, d)
    ka, ki, qb, vb, ga, gb, qat, vat, qit, wit, kbt = _in_proj(x, mod, norm1_g[0], w_in[0])
    ya = _dsa_t(qit, wit, ki, qat, ka, vat, _near_bias(rel_bias, DSA_T_TQ), DSA_T_TQ)
    yb = _stick_breaking(qb, kbt, vb)
    x1, hn2t, st = _mix(x, ya, yb, ga, gb, mod, norm2_g[0], w_proj_a[0], w_proj_b[0], w_out[0],
                        peer_wq[0], peer_sub_keys[0])
    stats, tops = _peer_select(st)
    ft = _peer_experts(hn2t, st, stats, tops, peer_u[0], peer_v[0])
    return _final(x1, ft, mod, final_g)
```
